```python
import jax
import jax.numpy as jnp
from jax import lax
import numpy as np

D_MODEL = 1024
BATCH = 2
SEQ = 16384
DEPTH = 2

CHUNK = 64
HEAD_DIM = 64
QBLOCK = 128
ROPE_THETA = 10000.0
EPS = 1e-6
NEG = -1e30

A_HEADS = 8
IDX_HEADS = 4
IDX_DIM = 64
IDX_TOPK_MAX = 256
B_HEADS = 8
B_PAST_CHUNKS = 8
B_REL_CLIP = 256
C_HEADS = 8
D_HEADS = 8
D_KV_HEADS = 2
D_GROUP = D_HEADS // D_KV_HEADS
D_WINDOW = 128
D_WINDOW_CHUNKS = -(-D_WINDOW // CHUNK)
D_FF = 2816
CONV_WIDTH = 3

N_EVEN = (DEPTH + 1) // 2
N_ODD = DEPTH // 2

EVEN_SPLITS = (A_HEADS * HEAD_DIM, A_HEADS * HEAD_DIM, A_HEADS * HEAD_DIM,
               IDX_HEADS * IDX_DIM, IDX_DIM, IDX_HEADS,
               B_HEADS * HEAD_DIM, B_HEADS * HEAD_DIM, B_HEADS * HEAD_DIM)
ODD_SPLITS = (C_HEADS * HEAD_DIM, C_HEADS * HEAD_DIM, C_HEADS * HEAD_DIM, C_HEADS,
              D_HEADS * HEAD_DIM, D_KV_HEADS * HEAD_DIM, D_KV_HEADS * HEAD_DIM)
EVEN_IN = sum(EVEN_SPLITS)
ODD_IN = sum(ODD_SPLITS)
EVEN_OUT = (A_HEADS + B_HEADS) * HEAD_DIM
ODD_OUT = (C_HEADS + D_HEADS) * HEAD_DIM

kernel_name = 'hybrid_chunk_causal_dsa_relpos_fox_swa'


def _split(t, sizes):
    return jnp.split(t, np.cumsum(sizes)[:-1].tolist(), axis=-1)


def rms_norm(x, g):
    xf = x.astype(jnp.float32)
    y = xf * lax.rsqrt(jnp.mean(xf * xf, axis=-1, keepdims=True) + EPS)
    return (y * g.astype(jnp.float32)).astype(x.dtype)


def layer_norm(x, g, b):
    xf = x.astype(jnp.float32)
    xc = xf - jnp.mean(xf, axis=-1, keepdims=True)
    y = xc * lax.rsqrt(jnp.mean(xc * xc, axis=-1, keepdims=True) + EPS)
    return (y * g.astype(jnp.float32) + b.astype(jnp.float32)).astype(x.dtype)


def rope_tables(seq, dim):
    inv = 1.0 / (ROPE_THETA ** (jnp.arange(0, dim, 2, dtype=jnp.float32) / dim))
    ang = jnp.arange(seq, dtype=jnp.float32)[:, None] * inv[None, :]
    return jnp.cos(ang), jnp.sin(ang)


def apply_rope(t, cos, sin):
    half = t.shape[-1] // 2
    shape = (1, t.shape[1]) + (1,) * (t.ndim - 3) + (half,)
    c = cos.reshape(shape)
    s = sin.reshape(shape)
    tf = t.astype(jnp.float32)
    t1, t2 = tf[..., :half], tf[..., half:]
    return jnp.concatenate([t1 * c - t2 * s, t2 * c + t1 * s], axis=-1).astype(t.dtype)


def dsa_sparse_attention(q, k, v, q_idx, k_idx, w_idx):
    bsz, seq, nh, dh = q.shape
    topk = min(IDX_TOPK_MAX, seq // 4)
    key_chunk = jnp.arange(seq) // CHUNK
    scale = dh ** -0.5

    def block(i):
        start = i * QBLOCK
        qb = lax.dynamic_slice_in_dim(q, start, QBLOCK, 1)
        qib = lax.dynamic_slice_in_dim(q_idx, start, QBLOCK, 1)
        wib = lax.dynamic_slice_in_dim(w_idx, start, QBLOCK, 1).astype(jnp.float32)
        q_chunk = (start + jnp.arange(QBLOCK)) // CHUNK
        admissible = key_chunk[None, :] <= q_chunk[:, None]
        rel = jax.nn.relu(jnp.einsum('bqhd,bsd->bqhs', qib, k_idx).astype(jnp.float32))
        score = jnp.einsum('bqhs,bqh->bqs', rel, wib)
        score = jnp.where(admissible[None], score, NEG)
        _, idx = lax.top_k(score, topk)
        valid = key_chunk[idx] <= q_chunk[None, :, None]
        k_sel = jax.vmap(lambda kb, ib: kb[ib])(k, idx)
        v_sel = jax.vmap(lambda vb, ib: vb[ib])(v, idx)
        logits = jnp.einsum('bqhd,bqkhd->bhqk', qb, k_sel).astype(jnp.float32) * scale
        logits = jnp.where(valid[:, None], logits, NEG)
        p = jax.nn.softmax(logits, axis=-1).astype(v.dtype)
        return jnp.einsum('bhqk,bqkhd->bqhd', p, v_sel)

    out = lax.map(block, jnp.arange(seq // QBLOCK))
    return jnp.moveaxis(out, 0, 1).reshape(bsz, seq, nh, dh)


def chunked_relpos_attention(q, k, v, rel_bias):
    bsz, seq, nh, dh = q.shape
    pad = B_PAST_CHUNKS * CHUNK
    band = pad + CHUNK
    kp = jnp.pad(k, ((0, 0), (pad, 0), (0, 0), (0, 0)))
    vp = jnp.pad(v, ((0, 0), (pad, 0), (0, 0), (0, 0)))
    kj = jnp.arange(band)
    dist = jnp.arange(CHUNK)[:, None] + pad - kj[None, :]
    bias = rel_bias[:, jnp.clip(dist, -B_REL_CLIP, B_REL_CLIP) + B_REL_CLIP].astype(jnp.float32)
    scale = dh ** -0.5

    def chunk(c):
        start = c * CHUNK
        qc = lax.dynamic_slice_in_dim(q, start, CHUNK, 1)
        kc = lax.dynamic_slice_in_dim(kp, start, band, 1)
        vc = lax.dynamic_slice_in_dim(vp, start, band, 1)
        valid = kj >= pad - start
        logits = jnp.einsum('bqhd,bshd->bhqs', qc, kc).astype(jnp.float32) * scale + bias[None]
        logits = jnp.where(valid, logits, NEG)
        p = jax.nn.softmax(logits, axis=-1).astype(v.dtype)
        return jnp.einsum('bhqs,bshd->bqhd', p, vc)

    out = lax.map(chunk, jnp.arange(seq // CHUNK))
    return jnp.moveaxis(out, 0, 1).reshape(bsz, seq, nh, dh)


def forgetting_attention(q, k, v, log_f):
    bsz, seq, nh, dh = q.shape
    f_cum = jnp.moveaxis(lax.cumsum(log_f, axis=1), 1, 2)
    pos = jnp.arange(seq)
    scale = dh ** -0.5

    def block(i):
        start = i * QBLOCK
        qb = lax.dynamic_slice_in_dim(q, start, QBLOCK, 1)
        fq = lax.dynamic_slice_in_dim(f_cum, start, QBLOCK, 2)
        causal = pos[None, :] <= (start + jnp.arange(QBLOCK))[:, None]
        logits = (jnp.einsum('bqhd,bshd->bhqs', qb, k).astype(jnp.float32) * scale
                  + fq[..., None] - f_cum[:, :, None, :])
        logits = jnp.where(causal, logits, NEG)
        p = jax.nn.softmax(logits, axis=-1).astype(v.dtype)
        return jnp.einsum('bhqs,bshd->bqhd', p, v)

    out = lax.map(block, jnp.arange(seq // QBLOCK))
    return jnp.moveaxis(out, 0, 1).reshape(bsz, seq, nh, dh)


def sink_window_gqa(q, k, v, sinks):
    bsz, seq, nh, dh = q.shape
    nb = seq // QBLOCK
    qb = q.reshape(bsz, nb, QBLOCK, D_KV_HEADS, D_GROUP, dh)
    kb = k.reshape(bsz, nb, QBLOCK, D_KV_HEADS, dh)
    vb = v.reshape(bsz, nb, QBLOCK, D_KV_HEADS, dh)
    kprev = jnp.pad(kb, ((0, 0), (1, 0), (0, 0), (0, 0), (0, 0)))[:, :-1]
    vprev = jnp.pad(vb, ((0, 0), (1, 0), (0, 0), (0, 0), (0, 0)))[:, :-1]
    kband = jnp.concatenate([kprev, kb], axis=2)
    vband = jnp.concatenate([vprev, vb], axis=2)
    sk = jnp.arange(2 * QBLOCK) - QBLOCK
    cdiff = jnp.arange(QBLOCK)[:, None] // CHUNK - sk[None, :] // CHUNK
    band_mask = (cdiff >= 0) & (cdiff <= D_WINDOW_CHUNKS)
    mask = band_mask[None] & ((jnp.arange(nb)[:, None, None] > 0) | (sk >= 0)[None, None, :])
    logits = jnp.einsum('bnqkgd,bnskd->bnkgqs', qb, kband).astype(jnp.float32) * (dh ** -0.5)
    logits = jnp.where(mask[None, :, None, None], logits, NEG)
    sink_col = jnp.broadcast_to(
        sinks.reshape(D_KV_HEADS, D_GROUP).astype(jnp.float32)[None, None, :, :, None, None],
        logits.shape[:-1] + (1,))
    p = jax.nn.softmax(jnp.concatenate([logits, sink_col], axis=-1), axis=-1)[..., :-1]
    out = jnp.einsum('bnkgqs,bnskd->bnqkgd', p.astype(v.dtype), vband)
    return out.reshape(bsz, seq, nh, dh)


def conv_gated_ffn(x, w_up, conv_w, conv_b, w_down):
    h = x @ w_up
    c = h.shape[-1]
    h = lax.conv_general_dilated(
        h, conv_w.reshape(CONV_WIDTH, 1, c).astype(h.dtype), window_strides=(1,),
        padding=[(CONV_WIDTH - 1, 0)], dimension_numbers=('NWC', 'WIO', 'NWC'),
        feature_group_count=c) + conv_b.astype(h.dtype)
    g, u = jnp.split(h, 2, axis=-1)
    return (jax.nn.silu(g) * u) @ w_down


def even_mixer(xn, w_in, w_out, k_ln_g, k_ln_b, rel_bias, rope_h, rope_i):
    bsz, seq, _ = xn.shape
    qa, ka, va, qi, ki, wi, qb, kb, vb = _split(xn @ w_in, EVEN_SPLITS)
    heads = lambda t, n: t.reshape(bsz, seq, n, HEAD_DIM)
    qa = apply_rope(heads(qa, A_HEADS), *rope_h)
    ka = apply_rope(heads(ka, A_HEADS), *rope_h)
    qi = apply_rope(qi.reshape(bsz, seq, IDX_HEADS, IDX_DIM), *rope_i)
    ki = apply_rope(layer_norm(ki, k_ln_g, k_ln_b), *rope_i)
    ya = dsa_sparse_attention(qa, ka, heads(va, A_HEADS), qi, ki, wi)
    yb = chunked_relpos_attention(heads(qb, B_HEADS), heads(kb, B_HEADS), heads(vb, B_HEADS), rel_bias)
    y = jnp.concatenate([ya.reshape(bsz, seq, -1), yb.reshape(bsz, seq, -1)], axis=-1)
    return y @ w_out


def odd_mixer(xn, w_in, w_out, forget_b, sinks, rope_h):
    bsz, seq, _ = xn.shape
    qc, kc, vc, fc, qd, kd, vd = _split(xn @ w_in, ODD_SPLITS)
    heads = lambda t, n: t.reshape(bsz, seq, n, HEAD_DIM)
    log_f = jax.nn.log_sigmoid(fc.astype(jnp.float32) + forget_b.astype(jnp.float32))
    yc = forgetting_attention(heads(qc, C_HEADS), heads(kc, C_HEADS), heads(vc, C_HEADS), log_f)
    yd = sink_window_gqa(apply_rope(heads(qd, D_HEADS), *rope_h),
                         apply_rope(heads(kd, D_KV_HEADS), *rope_h),
                         heads(vd, D_KV_HEADS), sinks)
    y = jnp.concatenate([yc.reshape(bsz, seq, -1), yd.reshape(bsz, seq, -1)], axis=-1)
    return y @ w_out


def setup_inputs(seed: int = 0) -> dict:
    key = jax.random.key(seed)
    ks = jax.random.split(key, 17)
    f32 = jnp.float32
    nrm = lambda k, shape, s: jax.random.normal(k, shape, f32) * s
    return {
        'x': nrm(ks[0], (BATCH, SEQ, D_MODEL), 1.0),
        'norm_mix_g': 1.0 + nrm(ks[1], (DEPTH, D_MODEL), 0.1),
        'norm_ffn_g': 1.0 + nrm(ks[2], (DEPTH, D_MODEL), 0.1),
        'norm_out_g': 1.0 + nrm(ks[3], (D_MODEL,), 0.1),
        'even_w_in': nrm(ks[4], (N_EVEN, D_MODEL, EVEN_IN), D_MODEL ** -0.5),
        'even_w_out': nrm(ks[5], (N_EVEN, EVEN_OUT, D_MODEL), EVEN_OUT ** -0.5),
        'idx_k_ln_g': 1.0 + nrm(ks[6], (N_EVEN, IDX_DIM), 0.1),
        'idx_k_ln_b': nrm(ks[7], (N_EVEN, IDX_DIM), 0.02),
        'rel_bias': nrm(ks[8], (N_EVEN, B_HEADS, 2 * B_REL_CLIP + 1), 0.5),
        'odd_w_in': nrm(ks[9], (N_ODD, D_MODEL, ODD_IN), D_MODEL ** -0.5),
        'odd_w_out': nrm(ks[10], (N_ODD, ODD_OUT, D_MODEL), ODD_OUT ** -0.5),
        'forget_b': jax.random.uniform(ks[11], (N_ODD, C_HEADS), f32, 1.0, 5.0),
        'sinks': nrm(ks[12], (N_ODD, D_HEADS), 1.0),
        'ffn_w_up': nrm(ks[13], (DEPTH, D_MODEL, 2 * D_FF), D_MODEL ** -0.5),
        'ffn_conv_w': nrm(ks[14], (DEPTH, CONV_WIDTH, 2 * D_FF), CONV_WIDTH ** -0.5),
        'ffn_conv_b': nrm(ks[15], (DEPTH, 2 * D_FF), 0.02),
        'ffn_w_down': nrm(ks[16], (DEPTH, D_FF, D_MODEL), D_FF ** -0.5),
    }


def reference(x, norm_mix_g, norm_ffn_g, norm_out_g, even_w_in, even_w_out, idx_k_ln_g, idx_k_ln_b,
              rel_bias, odd_w_in, odd_w_out, forget_b, sinks, ffn_w_up, ffn_conv_w, ffn_conv_b, ffn_w_down):
    seq = x.shape[1]
    rope_h = rope_tables(seq, HEAD_DIM)
    rope_i = rope_tables(seq, IDX_DIM)
    h = x
    for layer in range(DEPTH):
        j = layer // 2
        xn = rms_norm(h, norm_mix_g[layer])
        if layer % 2 == 0:
            mix = even_mixer(xn, even_w_in[j], even_w_out[j], idx_k_ln_g[j], idx_k_ln_b[j],
                             rel_bias[j], rope_h, rope_i)
        else:
            mix = odd_mixer(xn, odd_w_in[j], odd_w_out[j], forget_b[j], sinks[j], rope_h)
        h = h + mix
        h = h + conv_gated_ffn(rms_norm(h, norm_ffn_g[layer]), ffn_w_up[layer], ffn_conv_w[layer],
                               ffn_conv_b[layer], ffn_w_down[layer])
    return rms_norm(h, norm_out_g)
```

```python
import functools

import numpy as np
import jax
import jax.numpy as jnp
from jax import lax
from jax.experimental import pallas as pl
from jax.experimental.pallas import tpu as pltpu

F32 = jnp.float32
BF16 = jnp.bfloat16

D_MODEL = 1024
HEAD_DIM = 64
CHUNK = 64
ROPE_THETA = 10000.0
EPS = 1e-6
NEG = -1e30
N_HEADS = 8
IDX_HEADS = 4
IDX_DIM = 64
IDX_TOPK_MAX = 256
B_PAST_CHUNKS = 8
B_REL_CLIP = 256
D_KV_HEADS = 2
D_WINDOW_CHUNKS = 2
D_FF = 2816
HW = N_HEADS * HEAD_DIM
SCALE = HEAD_DIM ** -0.5

LANES = 128
VMEM_LIMIT = 56 * 1024 * 1024

TM_PROJ = 512
TQ = 256
SUB = 512
TKB_MAX = 2048
TF = 256
HALO = 16


def _cparams(n_axes):
    return pltpu.CompilerParams(
        dimension_semantics=("arbitrary",) * n_axes,
        vmem_limit_bytes=VMEM_LIMIT)


def _rms(x, g):
    ms = jnp.mean(x * x, axis=-1, keepdims=True)
    return x * lax.rsqrt(ms + EPS) * g


def _lane_iota(shape):
    return lax.broadcasted_iota(jnp.int32, shape, len(shape) - 1)


def _rope(t, cos, sin):
    w = t.shape[1]
    lane = _lane_iota(t.shape)
    fwd = pltpu.roll(t, 32, 1)
    bwd = pltpu.roll(t, w - 32, 1)
    rot = jnp.where((lane % HEAD_DIM) < (HEAD_DIM // 2), bwd, fwd)
    return t * cos + rot * sin


def _tile_lanes(t, reps):
    return t if reps == 1 else jnp.concatenate([t] * reps, axis=1)


def _expand_heads(t):
    pieces = []
    for j in range(t.shape[1] // LANES):
        p = t[:, j * LANES:(j + 1) * LANES]
        lo = _lane_iota(p.shape) < HEAD_DIM
        pieces.append(jnp.where(lo, p, 0.0))
        pieces.append(jnp.where(lo, 0.0, p))
    return jnp.concatenate(pieces, axis=1).astype(BF16)


def _expand_kv_pair(t):
    lo = _lane_iota(t.shape) < HEAD_DIM
    sw = pltpu.roll(t, HEAD_DIM, 1)
    pieces = [jnp.where(lo, t, 0.0), jnp.where(lo, 0.0, sw),
              jnp.where(lo, sw, 0.0), jnp.where(lo, 0.0, t)]
    return jnp.concatenate(pieces, axis=1).astype(BF16)


def _dot(a, b):
    return jnp.dot(a, b, preferred_element_type=F32)


def _dot_nt(a, b):
    return lax.dot_general(a, b, (((1,), (1,)), ((), ())), preferred_element_type=F32)


def _even_proj_kernel(x_ref, g_ref, w_ref, cos_ref, sin_ref, lng_ref, lnb_ref,
                      qa_ref, ka_ref, va_ref, qi_ref, ki_ref, wi_ref,
                      qb_ref, kb_ref, vb_ref):
    xn = _rms(x_ref[...], g_ref[...]).astype(BF16)
    cos = cos_ref[...]
    sin = sin_ref[...]
    cos4 = _tile_lanes(cos, 4)
    sin4 = _tile_lanes(sin, 4)

    def proj(a, b):
        return _dot(xn, w_ref[:, a:b])

    qa_ref[...] = (_rope(proj(0, 512), cos4, sin4) * SCALE).astype(BF16)
    ka_ref[...] = _expand_heads(_rope(proj(512, 1024), cos4, sin4))
    va_ref[...] = _expand_heads(proj(1024, 1536))
    qi_ref[...] = _rope(proj(1536, 1792), _tile_lanes(cos, 2), _tile_lanes(sin, 2)).astype(BF16)
    small = proj(1792, 2048)
    kraw = small[:, :LANES]
    wi_ref[...] = small[:, LANES:]
    valid = _lane_iota(kraw.shape) < IDX_DIM
    mean = jnp.sum(kraw, axis=-1, keepdims=True) * (1.0 / IDX_DIM)
    xc = jnp.where(valid, kraw - mean, 0.0)
    var = jnp.sum(xc * xc, axis=-1, keepdims=True) * (1.0 / IDX_DIM)
    kn = xc * lax.rsqrt(var + EPS) * lng_ref[...] + lnb_ref[...]
    ki_ref[...] = _rope(kn, cos, sin).astype(BF16)
    qb_ref[...] = (proj(2048, 2560) * SCALE).astype(BF16)
    kb_ref[...] = _expand_heads(proj(2560, 3072))
    vb_ref[...] = _expand_heads(proj(3072, 3584))


def _even_proj(x2, g, w, cos, sin, lng, lnb, seq):
    m = x2.shape[0]
    tm = min(TM_PROJ, seq)
    nseq = seq // tm
    row = lambda i: (i, 0)
    const = lambda i: (0, 0)
    tab = lambda i: (i % nseq, 0)
    widths = [(512, BF16), (1024, BF16), (1024, BF16), (256, BF16), (128, BF16),
              (128, F32), (512, BF16), (1024, BF16), (1024, BF16)]
    return pl.pallas_call(
        _even_proj_kernel,
        grid=(m // tm,),
        in_specs=[pl.BlockSpec((tm, D_MODEL), row),
                  pl.BlockSpec((1, D_MODEL), const),
                  pl.BlockSpec(w.shape, const),
                  pl.BlockSpec((tm, LANES), tab),
                  pl.BlockSpec((tm, LANES), tab),
                  pl.BlockSpec((1, LANES), const),
                  pl.BlockSpec((1, LANES), const)],
        out_specs=[pl.BlockSpec((tm, wd), row) for wd, _ in widths],
        out_shape=[jax.ShapeDtypeStruct((m, wd), dt) for wd, dt in widths],
        compiler_params=_cparams(1),
        name="even_proj",
    )(x2, g, w, cos, sin, lng, lnb)


def _odd_proj_kernel(x_ref, g_ref, w_ref, cos_ref, sin_ref, fb_ref, tril_ref,
                     qc_ref, kc_ref, vc_ref, f_ref, qd_ref, kd_ref, vd_ref,
                     carry_ref, *, nseq):
    i = pl.program_id(0)
    xn = _rms(x_ref[...], g_ref[...]).astype(BF16)
    cos = cos_ref[...]
    sin = sin_ref[...]

    def proj(a, b):
        return _dot(xn, w_ref[:, a:b])

    qc_ref[...] = (proj(0, 512) * SCALE).astype(BF16)
    kc_ref[...] = _expand_heads(proj(512, 1024))
    vc_ref[...] = _expand_heads(proj(1024, 1536))

    z = proj(1536, 1664) + fb_ref[...]
    lf = jnp.minimum(z, 0.0) - jnp.log(1.0 + jnp.exp(-jnp.abs(z)))
    lf = jnp.where(_lane_iota(lf.shape) < N_HEADS, lf, 0.0)
    hi = lf.astype(BF16)
    r1 = lf - hi.astype(F32)
    mid = r1.astype(BF16)
    lo = (r1 - mid.astype(F32)).astype(BF16)
    tril = tril_ref[...]
    csum = _dot(tril, hi) + _dot(tril, mid) + _dot(tril, lo)

    @pl.when(i % nseq == 0)
    def _():
        carry_ref[...] = jnp.zeros_like(carry_ref)

    fcum = csum + carry_ref[0:1, :]
    f_ref[...] = fcum
    carry_ref[...] = jnp.broadcast_to(fcum[fcum.shape[0] - 1:, :], carry_ref.shape)

    qd_ref[...] = (_rope(proj(1664, 2176), _tile_lanes(cos, 4), _tile_lanes(sin, 4))
                   * SCALE).astype(BF16)
    kv = proj(2176, 2432)
    kd_ref[...] = _expand_kv_pair(_rope(kv[:, :LANES], cos, sin))
    vd_ref[...] = _expand_kv_pair(kv[:, LANES:])


def _odd_proj(x2, g, w, cos, sin, fb, seq):
    m = x2.shape[0]
    tm = min(TM_PROJ, seq)
    nseq = seq // tm
    row = lambda i: (i, 0)
    const = lambda i: (0, 0)
    tab = lambda i: (i % nseq, 0)
    tril = jnp.tril(jnp.ones((tm, tm), F32)).astype(BF16)
    widths = [(512, BF16), (1024, BF16), (1024, BF16), (128, F32),
              (512, BF16), (512, BF16), (512, BF16)]
    return pl.pallas_call(
        functools.partial(_odd_proj_kernel, nseq=nseq),
        grid=(m // tm,),
        in_specs=[pl.BlockSpec((tm, D_MODEL), row),
                  pl.BlockSpec((1, D_MODEL), const),
                  pl.BlockSpec(w.shape, const),
                  pl.BlockSpec((tm, LANES), tab),
                  pl.BlockSpec((tm, LANES), tab),
                  pl.BlockSpec((1, LANES), const),
                  pl.BlockSpec((tm, tm), const)],
        out_specs=[pl.BlockSpec((tm, wd), row) for wd, _ in widths],
        out_shape=[jax.ShapeDtypeStruct((m, wd), dt) for wd, dt in widths],
        scratch_shapes=[pltpu.VMEM((8, LANES), F32)],
        compiler_params=_cparams(1),
        name="odd_proj",
    )(x2, g, w, cos, sin, fb, tril)


def _flash_heads(q_ref, k_ref, v_ref, rows, acc_ref, m_ref, l_ref, logit_fn):
    lo = None
    for jp in range(N_HEADS // 2):
        qp = q_ref[:, jp * LANES:(jp + 1) * LANES]
        pv_pair = None
        alphas = []
        for e in range(2):
            h = 2 * jp + e
            kh = k_ref[rows, h * LANES:(h + 1) * LANES]
            s, shift = logit_fn(h, _dot_nt(qp, kh))
            m_prev = m_ref[h]
            m_cur = jnp.max(s, axis=1, keepdims=True)
            if shift is not None:
                m_cur = m_cur + shift
            m_new = jnp.maximum(m_prev, m_cur)
            alpha = jnp.exp(m_prev - m_new)
            off = m_new if shift is None else m_new - shift
            p = jnp.exp(s - off)
            l_ref[h] = alpha * l_ref[h] + jnp.sum(p, axis=1, keepdims=True)
            m_ref[h] = m_new
            vh = v_ref[rows, h * LANES:(h + 1) * LANES]
            pv = _dot(p.astype(BF16), vh)
            pv_pair = pv if pv_pair is None else pv_pair + pv
            alphas.append(alpha)
        if lo is None:
            lo = _lane_iota(pv_pair.shape) < HEAD_DIM
        a_pair = jnp.where(lo, alphas[0], alphas[1])
        cols = slice(jp * LANES, (jp + 1) * LANES)
        acc_ref[:, cols] = acc_ref[:, cols] * a_pair + pv_pair


def _flash_init(acc_ref, m_ref, l_ref):
    acc_ref[...] = jnp.zeros_like(acc_ref)
    m_ref[...] = jnp.full(m_ref.shape, NEG, F32)
    l_ref[...] = jnp.zeros_like(l_ref)


def _flash_finish(o_ref, acc_ref, l_ref, extra_l=None):
    for jp in range(N_HEADS // 2):
        cols = slice(jp * LANES, (jp + 1) * LANES)
        l0 = l_ref[2 * jp]
        l1 = l_ref[2 * jp + 1]
        if extra_l is not None:
            l0 = l0 + extra_l[2 * jp]
            l1 = l1 + extra_l[2 * jp + 1]
        acc = acc_ref[:, cols]
        lo = _lane_iota(acc.shape) < HEAD_DIM
        o_ref[:, cols] = (acc / jnp.where(lo, l0, l1)).astype(o_ref.dtype)


def _kv_tiles(seq):
    tkb = min(TKB_MAX, seq)
    return tkb, seq // tkb, tkb // SUB


def _mixer_c_kernel(q_ref, k_ref, v_ref, fq_ref, fk_ref, o_ref, acc_ref, m_ref, l_ref,
                    *, tkb, nsub):
    i = pl.program_id(1)
    j = pl.program_id(2)
    q_lo = i * TQ
    last_sub = q_lo // SUB
    jmax = last_sub // nsub

    @pl.when(j == 0)
    def _():
        _flash_init(acc_ref, m_ref, l_ref)

    def step(c, masked):
        rows = pl.ds(pl.multiple_of(c * SUB, SUB), SUB)

        def logit_fn(h, s):
            s = s - fk_ref[h:h + 1, rows]
            if masked:
                kpos = j * tkb + c * SUB + lax.broadcasted_iota(jnp.int32, s.shape, 1)
                qpos = q_lo + lax.broadcasted_iota(jnp.int32, s.shape, 0)
                s = jnp.where(kpos <= qpos, s, NEG)
            return s, fq_ref[:, h:h + 1]

        _flash_heads(q_ref, k_ref, v_ref, rows, acc_ref, m_ref, l_ref, logit_fn)

    @pl.when(j <= jmax)
    def _():
        n_full = jnp.minimum(last_sub - j * nsub, nsub)

        def body(c, carry):
            step(c, False)
            return carry

        lax.fori_loop(0, n_full, body, 0)

        @pl.when(j == jmax)
        def _():
            step(last_sub - j * nsub, True)
            _flash_finish(o_ref, acc_ref, l_ref)


def _mixer_c(q, k, v, fq, fk):
    bsz, seq, _ = q.shape
    tkb, nkb, nsub = _kv_tiles(seq)
    kv_idx = lambda b, i, j: (b, jnp.minimum(j, (i * TQ) // tkb), 0)
    return pl.pallas_call(
        functools.partial(_mixer_c_kernel, tkb=tkb, nsub=nsub),
        grid=(bsz, seq // TQ, nkb),
        in_specs=[pl.BlockSpec((None, TQ, HW), lambda b, i, j: (b, i, 0)),
                  pl.BlockSpec((None, tkb, 2 * HW), kv_idx),
                  pl.BlockSpec((None, tkb, 2 * HW), kv_idx),
                  pl.BlockSpec((None, TQ, LANES), lambda b, i, j: (b, i, 0)),
                  pl.BlockSpec((None, N_HEADS, tkb),
                               lambda b, i, j: (b, 0, jnp.minimum(j, (i * TQ) // tkb)))],
        out_specs=pl.BlockSpec((None, TQ, HW), lambda b, i, j: (b, i, 0)),
        out_shape=jax.ShapeDtypeStruct((bsz, seq, HW), BF16),
        scratch_shapes=[pltpu.VMEM((TQ, HW), F32),
                        pltpu.VMEM((N_HEADS, TQ, 1), F32),
                        pltpu.VMEM((N_HEADS, TQ, 1), F32)],
        compiler_params=_cparams(3),
        name="mixer_c",
    )(q, k, v, fq, fk)


def _ukey_to_float(c):
    key = c ^ jnp.int32(-2 ** 31)
    bits = jnp.where(key >= 0, key, key ^ jnp.int32(0x7FFFFFFF))
    return pltpu.bitcast(bits, F32)


def _mixer_a_kernel(qi_ref, wi_ref, kie_ref, kio_ref, q_ref, k_ref, v_ref, tri_ref,
                    o_ref, s_ref, tau_ref, r_ref, cnt_ref, acc_ref, m_ref, l_ref,
                    *, tkb, nsub, topk):
    i = pl.program_id(1)
    j = pl.program_id(2)
    q_lo = i * TQ
    last_sub = q_lo // SUB
    jmax = last_sub // nsub
    n_sub = last_sub + 1

    def admissible(c_glob, shape):
        kpos = c_glob * SUB + lax.broadcasted_iota(jnp.int32, shape, 1)
        qpos = q_lo + lax.broadcasted_iota(jnp.int32, shape, 0)
        return (kpos // CHUNK) <= (qpos // CHUNK)

    @pl.when(j == 0)
    def _():
        _flash_init(acc_ref, m_ref, l_ref)
        cnt_ref[...] = jnp.zeros_like(cnt_ref)

        def score(c, masked):
            cols = pl.ds(pl.multiple_of(c * SUB, SUB), SUB)
            tot = None
            for pair in range(IDX_HEADS // 2):
                qp = qi_ref[:, pair * LANES:(pair + 1) * LANES]
                for e in range(2):
                    h = 2 * pair + e
                    kt = (kie_ref if e == 0 else kio_ref)[:, cols]
                    term = jnp.maximum(_dot(qp, kt), 0.0) * wi_ref[:, h:h + 1]
                    tot = term if tot is None else tot + term
            if masked:
                tot = jnp.where(admissible(c, tot.shape), tot, NEG)
            s_ref[:, cols] = tot

        def score_body(c, carry):
            score(c, False)
            return carry

        lax.fori_loop(0, last_sub, score_body, 0)
        score(last_sub, True)

        def count(cand, strict):
            cand = jnp.broadcast_to(cand, (TQ, LANES))

            def body(c, acc):
                for g in range(SUB // LANES):
                    blk = s_ref[:, pl.ds(pl.multiple_of(c * SUB + g * LANES, LANES), LANES)]
                    hit = (blk > cand) if strict else (blk >= cand)
                    acc = acc + jnp.where(hit, 1.0, 0.0)
                return acc
            acc = lax.fori_loop(0, n_sub, body, jnp.zeros((TQ, LANES), F32))
            return jnp.sum(acc, axis=1, keepdims=True)

        def bit_body(it, prefix):
            cand = prefix | lax.shift_left(jnp.int32(1), 31 - it)
            n_ge = count(_ukey_to_float(cand), False)
            return jnp.where(n_ge >= topk, cand, prefix)

        prefix = lax.fori_loop(0, 32, bit_body, jnp.zeros((TQ, 1), jnp.int32))
        tau = _ukey_to_float(prefix)
        tau_ref[...] = tau
        r_ref[...] = topk - count(tau, True)

    def step(c, masked):
        rows = pl.ds(pl.multiple_of(c * SUB, SUB), SUB)
        c_glob = j * nsub + c
        blk = s_ref[:, pl.ds(pl.multiple_of(c_glob * SUB, SUB), SUB)]
        tau = tau_ref[...]
        eq = blk == tau
        rank = cnt_ref[...] + _dot(jnp.where(eq, 1.0, 0.0).astype(BF16), tri_ref[...])
        bias = jnp.where(blk > tau, 0.0,
                         jnp.where(eq, jnp.where(rank <= r_ref[...], 0.0, NEG), NEG))
        if masked:
            bias = jnp.where(admissible(c_glob, bias.shape), bias, NEG)
        cnt_ref[...] = rank[:, SUB - 1:SUB]

        _flash_heads(q_ref, k_ref, v_ref, rows, acc_ref, m_ref, l_ref,
                     lambda h, s: (s + bias, None))

    @pl.when(j <= jmax)
    def _():
        n_full = jnp.minimum(last_sub - j * nsub, nsub)

        def body(c, carry):
            step(c, False)
            return carry

        lax.fori_loop(0, n_full, body, 0)

        @pl.when(j == jmax)
        def _():
            step(last_sub - j * nsub, True)
            _flash_finish(o_ref, acc_ref, l_ref)


def _mixer_a(qi, wi, kie, kio, q, k, v):
    bsz, seq, _ = q.shape
    tkb, nkb, nsub = _kv_tiles(seq)
    topk = min(IDX_TOPK_MAX, seq // 4)
    tri = jnp.triu(jnp.ones((SUB, SUB), F32)).astype(BF16)
    qmap = lambda b, i, j: (b, i, 0)
    kv_idx = lambda b, i, j: (b, jnp.minimum(j, (i * TQ) // tkb), 0)
    res = lambda b, i, j: (b, 0, 0)
    return pl.pallas_call(
        functools.partial(_mixer_a_kernel, tkb=tkb, nsub=nsub, topk=topk),
        grid=(bsz, seq // TQ, nkb),
        in_specs=[pl.BlockSpec((None, TQ, IDX_HEADS * IDX_DIM), qmap),
                  pl.BlockSpec((None, TQ, LANES), qmap),
                  pl.BlockSpec((None, LANES, seq), res),
                  pl.BlockSpec((None, LANES, seq), res),
                  pl.BlockSpec((None, TQ, HW), qmap),
                  pl.BlockSpec((None, tkb, 2 * HW), kv_idx),
                  pl.BlockSpec((None, tkb, 2 * HW), kv_idx),
                  pl.BlockSpec((SUB, SUB), lambda b, i, j: (0, 0))],
        out_specs=pl.BlockSpec((None, TQ, HW), qmap),
        out_shape=jax.ShapeDtypeStruct((bsz, seq, HW), BF16),
        scratch_shapes=[pltpu.VMEM((TQ, seq), F32),
                        pltpu.VMEM((TQ, 1), F32),
                        pltpu.VMEM((TQ, 1), F32),
                        pltpu.VMEM((TQ, 1), F32),
                        pltpu.VMEM((TQ, HW), F32),
                        pltpu.VMEM((N_HEADS, TQ, 1), F32),
                        pltpu.VMEM((N_HEADS, TQ, 1), F32)],
        compiler_params=_cparams(3),
        name="mixer_a",
    )(qi, wi, kie, kio, q, k, v, tri)


B_WIN = 3


def _mixer_b_kernel(q_ref, k0_ref, k1_ref, k2_ref, v0_ref, v1_ref, v2_ref, bias_ref, o_ref):
    i = pl.program_id(1)
    k_refs = (k0_ref, k1_ref, k2_ref)
    v_refs = (v0_ref, v1_ref, v2_ref)
    pad = [jnp.where(i - (B_WIN - 1 - a) >= 0, 0.0, NEG) for a in range(B_WIN)]
    for jp in range(N_HEADS // 2):
        qp = q_ref[:, jp * LANES:(jp + 1) * LANES]
        out_pair = None
        for e in range(2):
            h = 2 * jp + e
            hc = slice(h * LANES, (h + 1) * LANES)
            s = [_dot_nt(qp, k_refs[a][:, hc]) + bias_ref[h, :, a * TQ:(a + 1) * TQ] + pad[a]
                 for a in range(B_WIN)]
            m = functools.reduce(jnp.maximum, [jnp.max(t, axis=1, keepdims=True) for t in s])
            p = [jnp.exp(t - m) for t in s]
            l = functools.reduce(jnp.add, [jnp.sum(t, axis=1, keepdims=True) for t in p])
            pv = functools.reduce(
                jnp.add, [_dot(p[a].astype(BF16), v_refs[a][:, hc]) for a in range(B_WIN)])
            pv = pv / l
            out_pair = pv if out_pair is None else out_pair + pv
        o_ref[:, jp * LANES:(jp + 1) * LANES] = out_pair.astype(o_ref.dtype)


def _mixer_b(q, k, v, bias_tab):
    bsz, seq, _ = q.shape
    qmap = lambda b, i: (b, i, 0)
    kmaps = [functools.partial(lambda b, i, d: (b, jnp.maximum(i - d, 0), 0), d=B_WIN - 1 - a)
             for a in range(B_WIN)]
    kspec = [pl.BlockSpec((None, TQ, 2 * HW), km) for km in kmaps]
    return pl.pallas_call(
        _mixer_b_kernel,
        grid=(bsz, seq // TQ),
        in_specs=[pl.BlockSpec((None, TQ, HW), qmap)] + kspec + kspec
                 + [pl.BlockSpec(bias_tab.shape, lambda b, i: (0, 0, 0))],
        out_specs=pl.BlockSpec((None, TQ, HW), qmap),
        out_shape=jax.ShapeDtypeStruct((bsz, seq, HW), BF16),
        compiler_params=_cparams(2),
        name="mixer_b",
    )(q, k, k, k, v, v, v, bias_tab)


def _b_bias_table(rel_bias):
    t = np.arange(TQ)[:, None]
    jk = np.arange(B_WIN * TQ)[None, :]
    dist = t + (B_WIN - 1) * TQ - jk
    cdiff = (t // CHUNK + (B_WIN - 1) * TQ // CHUNK) - jk // CHUNK
    ok = (cdiff >= 0) & (cdiff <= B_PAST_CHUNKS)
    idx = np.clip(dist, -B_REL_CLIP, B_REL_CLIP) + B_REL_CLIP
    tab = rel_bias.astype(F32)[:, idx]
    return jnp.where(jnp.asarray(ok)[None], tab, NEG)


def _mixer_d_kernel(q_ref, k0_ref, k1_ref, v0_ref, v1_ref, sink_ref, o_ref):
    i = pl.program_id(1)
    k_refs = (k0_ref, k1_ref)
    v_refs = (v0_ref, v1_ref)
    qpos = TQ + lax.broadcasted_iota(jnp.int32, (TQ, TQ), 0)
    bias = []
    for a in range(2):
        kpos = a * TQ + lax.broadcasted_iota(jnp.int32, (TQ, TQ), 1)
        cdiff = qpos // CHUNK - kpos // CHUNK
        ok = (cdiff >= 0) & (cdiff <= D_WINDOW_CHUNKS)
        if a == 0:
            ok = ok & (i > 0)
        bias.append(jnp.where(ok, 0.0, NEG))
    group = N_HEADS // D_KV_HEADS
    for jp in range(N_HEADS // 2):
        qp = q_ref[:, jp * LANES:(jp + 1) * LANES]
        out_pair = None
        for e in range(2):
            h = 2 * jp + e
            piece = (h // group) * 2 + e
            hc = slice(piece * LANES, (piece + 1) * LANES)
            s = [_dot_nt(qp, k_refs[a][:, hc]) + bias[a] for a in range(2)]
            sink = sink_ref[h:h + 1, 0:1]
            m = jnp.maximum(jnp.maximum(jnp.max(s[0], axis=1, keepdims=True),
                                        jnp.max(s[1], axis=1, keepdims=True)), sink)
            p = [jnp.exp(t - m) for t in s]
            l = (jnp.sum(p[0], axis=1, keepdims=True) + jnp.sum(p[1], axis=1, keepdims=True)
                 + jnp.exp(sink - m))
            pv = (_dot(p[0].astype(BF16), v_refs[0][:, hc])
                  + _dot(p[1].astype(BF16), v_refs[1][:, hc])) / l
            out_pair = pv if out_pair is None else out_pair + pv
        o_ref[:, jp * LANES:(jp + 1) * LANES] = out_pair.astype(o_ref.dtype)


def _mixer_d(q, k, v, sinks):
    bsz, seq, _ = q.shape
    qmap = lambda b, i: (b, i, 0)
    prev = lambda b, i: (b, jnp.maximum(i - 1, 0), 0)
    sink_tab = jnp.broadcast_to(sinks.astype(F32)[:, None], (N_HEADS, LANES))
    return pl.pallas_call(
        _mixer_d_kernel,
        grid=(bsz, seq // TQ),
        in_specs=[pl.BlockSpec((None, TQ, HW), qmap),
                  pl.BlockSpec((None, TQ, HW), prev), pl.BlockSpec((None, TQ, HW), qmap),
                  pl.BlockSpec((None, TQ, HW), prev), pl.BlockSpec((None, TQ, HW), qmap),
                  pl.BlockSpec((N_HEADS, LANES), lambda b, i: (0, 0))],
        out_specs=pl.BlockSpec((None, TQ, HW), qmap),
        out_shape=jax.ShapeDtypeStruct((bsz, seq, HW), BF16),
        compiler_params=_cparams(2),
        name="mixer_d",
    )(q, k, k, v, v, sink_tab)


def _out_proj_kernel(h_ref, y0_ref, y1_ref, w0_ref, w1_ref, o_ref):
    o_ref[...] = h_ref[...] + _dot(y0_ref[...], w0_ref[...]) + _dot(y1_ref[...], w1_ref[...])


def _out_proj(h2, y0, y1, w0, w1, seq):
    m = h2.shape[0]
    tm = min(TM_PROJ, seq)
    row = lambda i: (i, 0)
    const = lambda i: (0, 0)
    return pl.pallas_call(
        _out_proj_kernel,
        grid=(m // tm,),
        in_specs=[pl.BlockSpec((tm, D_MODEL), row),
                  pl.BlockSpec((tm, HW), row), pl.BlockSpec((tm, HW), row),
                  pl.BlockSpec((HW, D_MODEL), const), pl.BlockSpec((HW, D_MODEL), const)],
        out_specs=pl.BlockSpec((tm, D_MODEL), row),
        out_shape=jax.ShapeDtypeStruct((m, D_MODEL), F32),
        compiler_params=_cparams(1),
        name="out_proj",
    )(h2, y0, y1, w0, w1)


def _ffn_kernel(h_ref, halo_ref, g_ref, wg_ref, wu_ref, cwg_ref, cwu_ref, cbg_ref, cbu_ref,
                wd_ref, go_ref, o_ref, xn_ref, xh_ref, acc_ref, hb_ref, *, nseq, final_norm):
    i = pl.program_id(0)
    f = pl.program_id(1)
    tm = h_ref.shape[0]

    @pl.when(f == 0)
    def _():
        xn_ref[...] = _rms(h_ref[...], g_ref[...]).astype(BF16)
        xh_ref[...] = _rms(halo_ref[...], g_ref[...]).astype(BF16)
        acc_ref[...] = jnp.zeros_like(acc_ref)

    halo_on = jnp.where(i % nseq == 0, 0.0, 1.0)

    def branch(w_ref, cw_ref, cb_ref):
        hb_ref[HALO:, :] = _dot(xn_ref[...], w_ref[...])
        hb_ref[:HALO, :] = _dot(xh_ref[...], w_ref[...]) * halo_on
        cw = cw_ref[...]
        return (hb_ref[pl.ds(HALO, tm), :] * cw[2:3, :]
                + hb_ref[pl.ds(HALO - 1, tm), :] * cw[1:2, :]
                + hb_ref[pl.ds(HALO - 2, tm), :] * cw[0:1, :]
                + cb_ref[...])

    yg = branch(wg_ref, cwg_ref, cbg_ref)
    yu = branch(wu_ref, cwu_ref, cbu_ref)
    act = (yg / (1.0 + jnp.exp(-yg))) * yu
    acc_ref[...] += _dot(act.astype(BF16), wd_ref[...])

    @pl.when(f == pl.num_programs(1) - 1)
    def _():
        out = h_ref[...] + acc_ref[...]
        if final_norm:
            out = _rms(out, go_ref[...])
        o_ref[...] = out


def _ffn(h2, g, w_up, conv_w, conv_b, w_down, g_out, seq, final_norm):
    m = h2.shape[0]
    tm = min(TM_PROJ, seq)
    nseq = seq // tm
    nf = D_FF // TF
    row = lambda i, f: (i, 0)
    const = lambda i, f: (0, 0)
    halo = lambda i, f: (jnp.maximum(i * (tm // HALO) - 1, 0), 0)
    gcol = lambda i, f: (0, f)
    ucol = lambda i, f: (0, nf + f)
    return pl.pallas_call(
        functools.partial(_ffn_kernel, nseq=nseq, final_norm=final_norm),
        grid=(m // tm, nf),
        in_specs=[pl.BlockSpec((tm, D_MODEL), row),
                  pl.BlockSpec((HALO, D_MODEL), halo),
                  pl.BlockSpec((1, D_MODEL), const),
                  pl.BlockSpec((D_MODEL, TF), gcol), pl.BlockSpec((D_MODEL, TF), ucol),
                  pl.BlockSpec((3, TF), gcol), pl.BlockSpec((3, TF), ucol),
                  pl.BlockSpec((1, TF), gcol), pl.BlockSpec((1, TF), ucol),
                  pl.BlockSpec((TF, D_MODEL), lambda i, f: (f, 0)),
                  pl.BlockSpec((1, D_MODEL), const)],
        out_specs=pl.BlockSpec((tm, D_MODEL), row),
        out_shape=jax.ShapeDtypeStruct((m, D_MODEL), F32),
        scratch_shapes=[pltpu.VMEM((tm, D_MODEL), BF16),
                        pltpu.VMEM((HALO, D_MODEL), BF16),
                        pltpu.VMEM((tm, D_MODEL), F32),
                        pltpu.VMEM((tm + HALO, TF), F32)],
        compiler_params=_cparams(2),
        name="ffn",
    )(h2, h2, g, w_up, w_up, conv_w, conv_w, conv_b, conv_b, w_down, g_out)


def _rope_tables(seq):
    inv = 1.0 / (ROPE_THETA ** (jnp.arange(0, HEAD_DIM, 2, dtype=F32) / HEAD_DIM))
    ang = jnp.arange(seq, dtype=F32)[:, None] * inv[None, :]
    c, s = jnp.cos(ang), jnp.sin(ang)
    cos = jnp.concatenate([c, c, c, c], axis=1)
    sin = jnp.concatenate([-s, s, -s, s], axis=1)
    return cos, sin


def _pad_cols(w, width):
    return jnp.pad(w, ((0, 0), (0, width - w.shape[1])))


def _even_weights(w_in):
    o = np.cumsum([0, 512, 512, 512, IDX_HEADS * IDX_DIM, IDX_DIM, IDX_HEADS, 512, 512, 512])
    sl = [w_in[:, o[n]:o[n + 1]] for n in range(9)]
    small = jnp.concatenate([_pad_cols(sl[4], LANES), _pad_cols(sl[5], LANES)], axis=1)
    return jnp.concatenate([sl[0], sl[1], sl[2], sl[3], small, sl[6], sl[7], sl[8]],
                           axis=1).astype(BF16)


def _odd_weights(w_in):
    o = np.cumsum([0, 512, 512, 512, N_HEADS, 512, D_KV_HEADS * HEAD_DIM, D_KV_HEADS * HEAD_DIM])
    sl = [w_in[:, o[n]:o[n + 1]] for n in range(7)]
    return jnp.concatenate([sl[0], sl[1], sl[2], _pad_cols(sl[3], LANES), sl[4], sl[5], sl[6]],
                           axis=1).astype(BF16)


def _row(v, width=None):
    v = v.astype(F32)[None, :]
    return v if width is None else _pad_cols(v, width)


def kernel(x, norm_mix_g, norm_ffn_g, norm_out_g, even_w_in, even_w_out, idx_k_ln_g, idx_k_ln_b,
           rel_bias, odd_w_in, odd_w_out, forget_b, sinks, ffn_w_up, ffn_conv_w, ffn_conv_b,
           ffn_w_down):
    bsz, seq, _ = x.shape
    m = bsz * seq
    depth = norm_mix_g.shape[0]
    assert seq % TQ == 0 and seq % min(TM_PROJ, seq) == 0 and seq % min(TKB_MAX, seq) == 0
    cos, sin = _rope_tables(seq)
    h = x.reshape(m, D_MODEL)
    as3 = lambda t: t.reshape(bsz, seq, t.shape[-1])

    for layer in range(depth):
        jj = layer // 2
        g_mix = _row(norm_mix_g[layer])
        if layer % 2 == 0:
            qa, ka, va, qi, ki, wi, qb, kb, vb = _even_proj(
                h, g_mix, _even_weights(even_w_in[jj]), cos, sin,
                _row(idx_k_ln_g[jj], LANES), _row(idx_k_ln_b[jj], LANES), seq)
            kit = jnp.swapaxes(as3(ki), 1, 2)
            kie, kio = kit, jnp.roll(kit, IDX_DIM, axis=1)
            y0 = _mixer_a(as3(qi), as3(wi), kie, kio, as3(qa), as3(ka), as3(va))
            y1 = _mixer_b(as3(qb), as3(kb), as3(vb), _b_bias_table(rel_bias[jj]))
            w_out = even_w_out[jj]
        else:
            qc, kc, vc, fcum, qd, kd, vd = _odd_proj(
                h, g_mix, _odd_weights(odd_w_in[jj]), cos, sin,
                _row(forget_b[jj], LANES), seq)
            fq = as3(fcum)
            fk = jnp.swapaxes(fq[:, :, :N_HEADS], 1, 2)
            y0 = _mixer_c(as3(qc), as3(kc), as3(vc), fq, fk)
            y1 = _mixer_d(as3(qd), as3(kd), as3(vd), sinks[jj])
            w_out = odd_w_out[jj]
        w_out = w_out.astype(BF16)
        h = _out_proj(h, y0.reshape(m, HW), y1.reshape(m, HW), w_out[:HW], w_out[HW:], seq)
        h = _ffn(h, _row(norm_ffn_g[layer]), ffn_w_up[layer].astype(BF16),
                 ffn_conv_w[layer].astype(F32), _row(ffn_conv_b[layer]),
                 ffn_w_down[layer].astype(BF16), _row(norm_out_g), seq,
                 final_norm=(layer == depth - 1))
    return h.reshape(bsz, seq, D_MODEL)
```

```python
import functools

import numpy as np
import jax
import jax.numpy as jnp
from jax import lax
from jax.experimental import pallas as pl
from jax.experimental.pallas import tpu as pltpu

F32 = jnp.float32
BF16 = jnp.bfloat16

D_MODEL = 1024
HEAD_DIM = 64
CHUNK = 64
ROPE_THETA = 10000.0
EPS = 1e-6
NEG = -1e30
N_HEADS = 8
IDX_HEADS = 4
IDX_DIM = 64
IDX_TOPK_MAX = 256
B_PAST_CHUNKS = 8
B_REL_CLIP = 256
D_KV_HEADS = 2
D_WINDOW_CHUNKS = 2
D_FF = 2816
HW = N_HEADS * HEAD_DIM
SCALE = HEAD_DIM ** -0.5
LOG2E = 1.4426950408889634
QSCALE = SCALE * LOG2E

LANES = 128
VMEM_LIMIT = 56 * 1024 * 1024

TM_PROJ = 512
TQ = 256
SUB = 512
TKB_MAX = 2048
TF = 256
HALO = 16


def _cparams(n_axes):
    return pltpu.CompilerParams(
        dimension_semantics=("arbitrary",) * n_axes,
        vmem_limit_bytes=VMEM_LIMIT)


def _rms(x, g):
    ms = jnp.mean(x * x, axis=-1, keepdims=True)
    return x * lax.rsqrt(ms + EPS) * g


def _lane_iota(shape):
    return lax.broadcasted_iota(jnp.int32, shape, len(shape) - 1)


def _rope(t, cos, sin):
    w = t.shape[1]
    lane = _lane_iota(t.shape)
    fwd = pltpu.roll(t, 32, 1)
    bwd = pltpu.roll(t, w - 32, 1)
    rot = jnp.where((lane % HEAD_DIM) < (HEAD_DIM // 2), bwd, fwd)
    return t * cos + rot * sin


def _tile_lanes(t, reps):
    return t if reps == 1 else jnp.concatenate([t] * reps, axis=1)


def _head_pieces(p, sw=None):
    lane = _lane_iota(p.shape)
    lo = lane < HEAD_DIM
    q = p if sw is None else sw
    even = jnp.where(lane == HEAD_DIM, 1.0, jnp.where(lo, p, 0.0))
    odd = jnp.where(lane == 0, 1.0, jnp.where(lo, 0.0, q))
    return even, odd


def _expand_heads(t, ones_lane=False):
    pieces = []
    for j in range(t.shape[1] // LANES):
        p = t[:, j * LANES:(j + 1) * LANES]
        if ones_lane:
            pieces.extend(_head_pieces(p))
        else:
            lo = _lane_iota(p.shape) < HEAD_DIM
            pieces.append(jnp.where(lo, p, 0.0))
            pieces.append(jnp.where(lo, 0.0, p))
    return jnp.concatenate(pieces, axis=1).astype(BF16)


def _expand_kv_pair(t, ones_lane=False):
    lo = _lane_iota(t.shape) < HEAD_DIM
    sw = pltpu.roll(t, HEAD_DIM, 1)
    if ones_lane:
        e0, o0 = _head_pieces(t, sw)
        e1, o1 = _head_pieces(sw, t)
        pieces = [e0, o0, e1, o1]
    else:
        pieces = [jnp.where(lo, t, 0.0), jnp.where(lo, 0.0, sw),
                  jnp.where(lo, sw, 0.0), jnp.where(lo, 0.0, t)]
    return jnp.concatenate(pieces, axis=1).astype(BF16)


def _dot(a, b):
    return jnp.dot(a, b, preferred_element_type=F32)


def _dot_nt(a, b):
    return lax.dot_general(a, b, (((1,), (1,)), ((), ())), preferred_element_type=F32)


def _even_proj_kernel(x_ref, g_ref, w_ref, cos_ref, sin_ref, lng_ref, lnb_ref,
                      qa_ref, ka_ref, va_ref, qi_ref, ki_ref, wi_ref,
                      qb_ref, kb_ref, vb_ref):
    xn = _rms(x_ref[...], g_ref[...]).astype(BF16)
    cos = cos_ref[...]
    sin = sin_ref[...]
    cos4 = _tile_lanes(cos, 4)
    sin4 = _tile_lanes(sin, 4)

    def proj(a, b):
        return _dot(xn, w_ref[:, a:b])

    qa_ref[...] = (_rope(proj(0, 512), cos4, sin4) * QSCALE).astype(BF16)
    ka_ref[...] = _expand_heads(_rope(proj(512, 1024), cos4, sin4))
    va_ref[...] = _expand_heads(proj(1024, 1536), ones_lane=True)
    qi_ref[...] = _rope(proj(1536, 1792), _tile_lanes(cos, 2), _tile_lanes(sin, 2)).astype(BF16)
    small = proj(1792, 2048)
    kraw = small[:, :LANES]
    wi_ref[...] = small[:, LANES:]
    valid = _lane_iota(kraw.shape) < IDX_DIM
    mean = jnp.sum(kraw, axis=-1, keepdims=True) * (1.0 / IDX_DIM)
    xc = jnp.where(valid, kraw - mean, 0.0)
    var = jnp.sum(xc * xc, axis=-1, keepdims=True) * (1.0 / IDX_DIM)
    kn = xc * lax.rsqrt(var + EPS) * lng_ref[...] + lnb_ref[...]
    ki_ref[...] = _rope(kn, cos, sin).astype(BF16)
    qb_ref[...] = (proj(2048, 2560) * QSCALE).astype(BF16)
    kb_ref[...] = _expand_heads(proj(2560, 3072))
    vb_ref[...] = _expand_heads(proj(3072, 3584), ones_lane=True)


def _even_proj(x2, g, w, cos, sin, lng, lnb, seq):
    m = x2.shape[0]
    tm = min(TM_PROJ, seq)
    nseq = seq // tm
    row = lambda i: (i, 0)
    const = lambda i: (0, 0)
    tab = lambda i: (i % nseq, 0)
    widths = [(512, BF16), (1024, BF16), (1024, BF16), (256, BF16), (128, BF16),
              (128, F32), (512, BF16), (1024, BF16), (1024, BF16)]
    return pl.pallas_call(
        _even_proj_kernel,
        grid=(m // tm,),
        in_specs=[pl.BlockSpec((tm, D_MODEL), row),
                  pl.BlockSpec((1, D_MODEL), const),
                  pl.BlockSpec(w.shape, const),
                  pl.BlockSpec((tm, LANES), tab),
                  pl.BlockSpec((tm, LANES), tab),
                  pl.BlockSpec((1, LANES), const),
                  pl.BlockSpec((1, LANES), const)],
        out_specs=[pl.BlockSpec((tm, wd), row) for wd, _ in widths],
        out_shape=[jax.ShapeDtypeStruct((m, wd), dt) for wd, dt in widths],
        compiler_params=_cparams(1),
        name="even_proj",
    )(x2, g, w, cos, sin, lng, lnb)


def _odd_proj_kernel(x_ref, g_ref, w_ref, cos_ref, sin_ref, fb_ref, tril_ref,
                     qc_ref, kc_ref, vc_ref, f_ref, qd_ref, kd_ref, vd_ref,
                     carry_ref, *, nseq):
    i = pl.program_id(0)
    xn = _rms(x_ref[...], g_ref[...]).astype(BF16)
    cos = cos_ref[...]
    sin = sin_ref[...]

    def proj(a, b):
        return _dot(xn, w_ref[:, a:b])

    qc_ref[...] = (proj(0, 512) * QSCALE).astype(BF16)
    kc_ref[...] = _expand_heads(proj(512, 1024))
    vc_ref[...] = _expand_heads(proj(1024, 1536), ones_lane=True)

    z = proj(1536, 1664) + fb_ref[...]
    lf = jnp.minimum(z, 0.0) - jnp.log(1.0 + jnp.exp(-jnp.abs(z)))
    lf = jnp.where(_lane_iota(lf.shape) < N_HEADS, lf, 0.0)
    hi = lf.astype(BF16)
    r1 = lf - hi.astype(F32)
    mid = r1.astype(BF16)
    lo = (r1 - mid.astype(F32)).astype(BF16)
    tril = tril_ref[...]
    csum = _dot(tril, hi) + _dot(tril, mid) + _dot(tril, lo)

    @pl.when(i % nseq == 0)
    def _():
        carry_ref[...] = jnp.zeros_like(carry_ref)

    fcum = csum + carry_ref[0:1, :]
    f_ref[...] = fcum * LOG2E
    carry_ref[...] = jnp.broadcast_to(fcum[fcum.shape[0] - 1:, :], carry_ref.shape)

    qd_ref[...] = (_rope(proj(1664, 2176), _tile_lanes(cos, 4), _tile_lanes(sin, 4))
                   * QSCALE).astype(BF16)
    kv = proj(2176, 2432)
    kd_ref[...] = _expand_kv_pair(_rope(kv[:, :LANES], cos, sin))
    vd_ref[...] = _expand_kv_pair(kv[:, LANES:], ones_lane=True)


def _odd_proj(x2, g, w, cos, sin, fb, seq):
    m = x2.shape[0]
    tm = min(TM_PROJ, seq)
    nseq = seq // tm
    row = lambda i: (i, 0)
    const = lambda i: (0, 0)
    tab = lambda i: (i % nseq, 0)
    tril = jnp.tril(jnp.ones((tm, tm), F32)).astype(BF16)
    widths = [(512, BF16), (1024, BF16), (1024, BF16), (128, F32),
              (512, BF16), (512, BF16), (512, BF16)]
    return pl.pallas_call(
        functools.partial(_odd_proj_kernel, nseq=nseq),
        grid=(m // tm,),
        in_specs=[pl.BlockSpec((tm, D_MODEL), row),
                  pl.BlockSpec((1, D_MODEL), const),
                  pl.BlockSpec(w.shape, const),
                  pl.BlockSpec((tm, LANES), tab),
                  pl.BlockSpec((tm, LANES), tab),
                  pl.BlockSpec((1, LANES), const),
                  pl.BlockSpec((tm, tm), const)],
        out_specs=[pl.BlockSpec((tm, wd), row) for wd, _ in widths],
        out_shape=[jax.ShapeDtypeStruct((m, wd), dt) for wd, dt in widths],
        scratch_shapes=[pltpu.VMEM((8, LANES), F32)],
        compiler_params=_cparams(1),
        name="odd_proj",
    )(x2, g, w, cos, sin, fb, tril)


def _denominator_lane(h):
    return HEAD_DIM if h % 2 == 0 else 0


def _flash_heads(q_ref, k_ref, v_ref, rows, acc_ref, m_ref, logit_fn):
    nrep = SUB // LANES

    def qk(h):
        jp = h // 2
        return _dot_nt(q_ref[:, jp * LANES:(jp + 1) * LANES],
                       k_ref[rows, h * LANES:(h + 1) * LANES])

    s_next = qk(0)
    for h in range(N_HEADS):
        s_raw = s_next
        if h + 1 < N_HEADS:
            s_next = qk(h + 1)
        s, shift = logit_fn(h, s_raw)
        m_prev = m_ref[h]
        m_cur = jnp.max(s, axis=1, keepdims=True)
        if shift is not None:
            m_cur = m_cur + shift
        m_new = jnp.maximum(m_prev, m_cur)
        alpha = jnp.exp2(m_prev - m_new)
        off = m_new if shift is None else m_new - shift
        p = jnp.exp2(s - _tile_lanes(off, nrep))
        m_ref[h] = m_new
        cols = slice(h * LANES, (h + 1) * LANES)
        acc_ref[:, cols] = acc_ref[:, cols] * alpha + _dot(p.astype(BF16), v_ref[rows, cols])


def _flash_init(acc_ref, m_ref):
    acc_ref[...] = jnp.zeros_like(acc_ref)
    m_ref[...] = jnp.full(m_ref.shape, NEG, F32)


def _pair_output(even, odd, l_even, l_odd):
    lo = _lane_iota(even.shape) < HEAD_DIM
    return jnp.where(lo, even / l_even, odd / l_odd)


def _flash_finish(o_ref, acc_ref):
    for jp in range(N_HEADS // 2):
        even = acc_ref[:, (2 * jp) * LANES:(2 * jp + 1) * LANES]
        odd = acc_ref[:, (2 * jp + 1) * LANES:(2 * jp + 2) * LANES]
        le = even[:, _denominator_lane(0):_denominator_lane(0) + 1]
        lo_ = odd[:, _denominator_lane(1):_denominator_lane(1) + 1]
        o_ref[:, jp * LANES:(jp + 1) * LANES] = _pair_output(even, odd, le, lo_).astype(o_ref.dtype)


def _kv_tiles(seq):
    tkb = min(TKB_MAX, seq)
    return tkb, seq // tkb, tkb // SUB


def _mixer_c_kernel(q_ref, k_ref, v_ref, fq_ref, fk_ref, o_ref, acc_ref, m_ref, fqr_ref,
                    *, tkb, nsub):
    i = pl.program_id(1)
    j = pl.program_id(2)
    q_lo = i * TQ
    last_sub = q_lo // SUB
    jmax = last_sub // nsub

    @pl.when(j == 0)
    def _():
        _flash_init(acc_ref, m_ref)
        for h in range(N_HEADS):
            fqr_ref[h] = jnp.broadcast_to(fq_ref[:, h:h + 1], (TQ, LANES))

    def step(c, masked):
        rows = pl.ds(pl.multiple_of(c * SUB, SUB), SUB)

        def logit_fn(h, s):
            s = s - fk_ref[h:h + 1, rows]
            if masked:
                kpos = j * tkb + c * SUB + lax.broadcasted_iota(jnp.int32, s.shape, 1)
                qpos = q_lo + lax.broadcasted_iota(jnp.int32, s.shape, 0)
                s = jnp.where(kpos <= qpos, s, NEG)
            return s, fqr_ref[h]

        _flash_heads(q_ref, k_ref, v_ref, rows, acc_ref, m_ref, logit_fn)

    @pl.when(j <= jmax)
    def _():
        n_full = jnp.minimum(last_sub - j * nsub, nsub)

        def body(c, carry):
            step(c, False)
            return carry

        lax.fori_loop(0, n_full, body, 0)

        @pl.when(j == jmax)
        def _():
            step(last_sub - j * nsub, True)
            _flash_finish(o_ref, acc_ref)


def _mixer_c(q, k, v, fq, fk):
    bsz, seq, _ = q.shape
    tkb, nkb, nsub = _kv_tiles(seq)
    kv_idx = lambda b, i, j: (b, jnp.minimum(j, (i * TQ) // tkb), 0)
    return pl.pallas_call(
        functools.partial(_mixer_c_kernel, tkb=tkb, nsub=nsub),
        grid=(bsz, seq // TQ, nkb),
        in_specs=[pl.BlockSpec((None, TQ, HW), lambda b, i, j: (b, i, 0)),
                  pl.BlockSpec((None, tkb, 2 * HW), kv_idx),
                  pl.BlockSpec((None, tkb, 2 * HW), kv_idx),
                  pl.BlockSpec((None, TQ, LANES), lambda b, i, j: (b, i, 0)),
                  pl.BlockSpec((None, N_HEADS, tkb),
                               lambda b, i, j: (b, 0, jnp.minimum(j, (i * TQ) // tkb)))],
        out_specs=pl.BlockSpec((None, TQ, HW), lambda b, i, j: (b, i, 0)),
        out_shape=jax.ShapeDtypeStruct((bsz, seq, HW), BF16),
        scratch_shapes=[pltpu.VMEM((TQ, 2 * HW), F32),
                        pltpu.VMEM((N_HEADS, TQ, LANES), F32),
                        pltpu.VMEM((N_HEADS, TQ, LANES), F32)],
        compiler_params=_cparams(3),
        name="mixer_c",
    )(q, k, v, fq, fk)


def _key_flip(bits):
    return jnp.where(bits >= 0, bits, bits ^ jnp.int32(0x7FFFFFFF))


def _float_to_key(f):
    return _key_flip(pltpu.bitcast(f, jnp.int32))


def _key_to_float(k):
    return pltpu.bitcast(_key_flip(k), F32)


def _mixer_a_kernel(qi_ref, wi_ref, kie_ref, kio_ref, q_ref, k_ref, v_ref, tri_ref, zero_ref,
                    o_ref, s_ref, gmax_ref, cnt_ref, acc_ref, m_ref,
                    *, tkb, nsub, topk):
    i = pl.program_id(1)
    j = pl.program_id(2)
    q_lo = i * TQ
    last_sub = q_lo // SUB
    jmax = last_sub // nsub
    n_sub = last_sub + 1
    nrep = SUB // LANES
    half = TQ // 2

    def admissible(c_glob, shape):
        kpos = c_glob * SUB + lax.broadcasted_iota(jnp.int32, shape, 1)
        qpos = q_lo + lax.broadcasted_iota(jnp.int32, shape, 0)
        return (kpos // CHUNK) <= (qpos // CHUNK)

    def s_load(c):
        return jnp.concatenate([s_ref[c * nrep + t] for t in range(nrep)], axis=1)

    def s_store(c, val):
        for t in range(nrep):
            s_ref[c * nrep + t] = val[:, t * LANES:(t + 1) * LANES]

    @pl.when(j == 0)
    def _():
        _flash_init(acc_ref, m_ref)
        cnt_ref[...] = jnp.zeros_like(cnt_ref)
        gmax_ref[...] = jnp.full(gmax_ref.shape, -jnp.inf, F32)

        def score(c, masked):
            cols = pl.ds(pl.multiple_of(c * SUB, SUB), SUB)
            tot = None
            for pair in range(IDX_HEADS // 2):
                qp = qi_ref[:, pair * LANES:(pair + 1) * LANES]
                for e in range(2):
                    h = 2 * pair + e
                    kt = (kie_ref if e == 0 else kio_ref)[:, cols]
                    term = jnp.maximum(_dot(qp, kt), 0.0) * wi_ref[:, h:h + 1]
                    tot = term if tot is None else tot + term
            if masked:
                tot = jnp.where(admissible(c, tot.shape), tot, NEG)
            s_store(c, tot)
            for par in range(2):
                g = gmax_ref[par]
                for t in range(par, nrep, 2):
                    g = jnp.maximum(g, tot[:, t * LANES:(t + 1) * LANES])
                gmax_ref[par] = g

        def score_body(c, carry):
            score(c, False)
            return carry

        lax.fori_loop(0, last_sub, score_body, 0)
        score(last_sub, True)

        def count(cand, strict):
            outs = []
            for hh in range(2):
                r0 = hh * half
                ch = cand[r0:r0 + half]

                def body(c, acc):
                    for t in range(nrep):
                        blk = s_ref[c * nrep + t, r0:r0 + half, :]
                        hit = (blk > ch) if strict else (blk >= ch)
                        acc = acc + jnp.where(hit, 1.0, 0.0)
                    return acc

                acc = lax.fori_loop(0, n_sub, body, jnp.zeros((half, LANES), F32))
                outs.append(jnp.broadcast_to(jnp.sum(acc, axis=1, keepdims=True), (half, LANES)))
            return jnp.concatenate(outs, axis=0)

        ga = gmax_ref[0]
        gb = gmax_ref[1]
        lo_f = jnp.min(jnp.minimum(ga, gb), axis=1, keepdims=True)
        hi_f = jnp.max(jnp.maximum(ga, gb), axis=1, keepdims=True)
        lo0 = jnp.broadcast_to(_float_to_key(lo_f), (TQ, LANES))
        hi0 = jnp.broadcast_to(_float_to_key(hi_f), (TQ, LANES))

        def unresolved(st):
            lo, hi = st
            return jnp.max(jnp.where(lo < hi, 1.0, 0.0)) > 0.5

        def bisect(st):
            lo, hi = st
            mid = (lo >> 1) + (hi >> 1) + ((lo | hi) & 1)
            n_ge = count(_key_to_float(mid), False)
            ge = n_ge >= topk
            return (jnp.where(ge, mid, lo),
                    jnp.where(n_ge == topk, mid, jnp.where(ge, hi, mid - 1)))

        lo, _ = lax.while_loop(unresolved, bisect, (lo0, hi0))
        tau1 = _key_to_float(lo)
        keep1 = topk - count(tau1, True)
        tau = _tile_lanes(tau1, nrep)
        keep = _tile_lanes(keep1, nrep)

        def to_mask(c, masked):
            blk = s_load(c)
            eq = blk == tau
            rank = (_tile_lanes(cnt_ref[...], nrep)
                    + _dot(jnp.where(eq, 1.0, 0.0).astype(BF16), tri_ref[...]))
            keep_tie = jnp.where(rank <= keep, 0.0, NEG)
            bias = jnp.where(blk > tau, 0.0, jnp.where(eq, keep_tie, NEG))
            if masked:
                bias = jnp.where(admissible(c, bias.shape), bias, NEG)
            cnt_ref[...] = jnp.broadcast_to(rank[:, SUB - 1:SUB], (TQ, LANES))
            s_store(c, bias)

        def mask_body(c, carry):
            to_mask(c, False)
            return carry

        lax.fori_loop(0, last_sub, mask_body, 0)
        to_mask(last_sub, True)

    @pl.when(j <= jmax)
    def _():
        def body(c, carry):
            rows = pl.ds(pl.multiple_of(c * SUB, SUB), SUB)
            _flash_heads(q_ref, k_ref, v_ref, rows, acc_ref, m_ref,
                         lambda h, s: (s + s_load(j * nsub + c + zero_ref[h]), None))
            return carry

        lax.fori_loop(0, jnp.minimum(n_sub - j * nsub, nsub), body, 0)

        @pl.when(j == jmax)
        def _():
            _flash_finish(o_ref, acc_ref)


def _mixer_a(qi, wi, kie, kio, q, k, v):
    bsz, seq, _ = q.shape
    tkb, nkb, nsub = _kv_tiles(seq)
    topk = min(IDX_TOPK_MAX, seq // 4)
    tri = jnp.triu(jnp.ones((SUB, SUB), F32)).astype(BF16)
    qmap = lambda b, i, j: (b, i, 0)
    kv_idx = lambda b, i, j: (b, jnp.minimum(j, (i * TQ) // tkb), 0)
    res = lambda b, i, j: (b, 0, 0)
    return pl.pallas_call(
        functools.partial(_mixer_a_kernel, tkb=tkb, nsub=nsub, topk=topk),
        grid=(bsz, seq // TQ, nkb),
        in_specs=[pl.BlockSpec((None, TQ, IDX_HEADS * IDX_DIM), qmap),
                  pl.BlockSpec((None, TQ, LANES), qmap),
                  pl.BlockSpec((None, LANES, seq), res),
                  pl.BlockSpec((None, LANES, seq), res),
                  pl.BlockSpec((None, TQ, HW), qmap),
                  pl.BlockSpec((None, tkb, 2 * HW), kv_idx),
                  pl.BlockSpec((None, tkb, 2 * HW), kv_idx),
                  pl.BlockSpec((SUB, SUB), lambda b, i, j: (0, 0)),
                  pl.BlockSpec(memory_space=pltpu.SMEM)],
        out_specs=pl.BlockSpec((None, TQ, HW), qmap),
        out_shape=jax.ShapeDtypeStruct((bsz, seq, HW), BF16),
        scratch_shapes=[pltpu.VMEM((seq // LANES, TQ, LANES), F32),
                        pltpu.VMEM((2, TQ, LANES), F32),
                        pltpu.VMEM((TQ, LANES), F32),
                        pltpu.VMEM((TQ, 2 * HW), F32),
                        pltpu.VMEM((N_HEADS, TQ, LANES), F32)],
        compiler_params=_cparams(3),
        name="mixer_a",
    )(qi, wi, kie, kio, q, k, v, tri, jnp.zeros((N_HEADS,), jnp.int32))


B_WIN = 3


def _window_heads(q_ref, o_ref, n_win, k_of, v_of, bias_of, sink_of=None):
    def scores(h):
        jp = h // 2
        qp = q_ref[:, jp * LANES:(jp + 1) * LANES]
        return [_dot_nt(qp, k_of(h, a)) + bias_of(h, a) for a in range(n_win)]

    s_next = scores(0)
    done = []
    for h in range(N_HEADS):
        s = s_next
        if h + 1 < N_HEADS:
            s_next = scores(h + 1)
        m = functools.reduce(jnp.maximum, [jnp.max(t, axis=1, keepdims=True) for t in s])
        if sink_of is not None:
            m = jnp.maximum(m, sink_of(h))
        pv = functools.reduce(
            jnp.add, [_dot(jnp.exp2(s[a] - m).astype(BF16), v_of(h, a)) for a in range(n_win)])
        lane = _denominator_lane(h)
        l = pv[:, lane:lane + 1]
        if sink_of is not None:
            l = l + jnp.exp2(sink_of(h) - m)
        done.append((pv, l))
        if h % 2 == 1:
            (pe, le), (po, lo_) = done[-2], done[-1]
            jp = h // 2
            o_ref[:, jp * LANES:(jp + 1) * LANES] = _pair_output(pe, po, le, lo_).astype(o_ref.dtype)


def _mixer_b_kernel(q_ref, k0_ref, k1_ref, k2_ref, v0_ref, v1_ref, v2_ref, bias_ref, o_ref):
    i = pl.program_id(1)
    k_refs = (k0_ref, k1_ref, k2_ref)
    v_refs = (v0_ref, v1_ref, v2_ref)
    pad = [jnp.where(i - (B_WIN - 1 - a) >= 0, 0.0, NEG) for a in range(B_WIN)]
    hc = lambda h: slice(h * LANES, (h + 1) * LANES)
    _window_heads(q_ref, o_ref, B_WIN,
                  lambda h, a: k_refs[a][:, hc(h)],
                  lambda h, a: v_refs[a][:, hc(h)],
                  lambda h, a: bias_ref[h, :, a * TQ:(a + 1) * TQ] + pad[a])


def _mixer_b(q, k, v, bias_tab):
    bsz, seq, _ = q.shape
    qmap = lambda b, i: (b, i, 0)
    kmaps = [functools.partial(lambda b, i, d: (b, jnp.maximum(i - d, 0), 0), d=B_WIN - 1 - a)
             for a in range(B_WIN)]
    kspec = [pl.BlockSpec((None, TQ, 2 * HW), km) for km in kmaps]
    return pl.pallas_call(
        _mixer_b_kernel,
        grid=(bsz, seq // TQ),
        in_specs=[pl.BlockSpec((None, TQ, HW), qmap)] + kspec + kspec
                 + [pl.BlockSpec(bias_tab.shape, lambda b, i: (0, 0, 0))],
        out_specs=pl.BlockSpec((None, TQ, HW), qmap),
        out_shape=jax.ShapeDtypeStruct((bsz, seq, HW), BF16),
        compiler_params=_cparams(2),
        name="mixer_b",
    )(q, k, k, k, v, v, v, bias_tab)


def _b_bias_table(rel_bias):
    win = B_WIN * TQ
    span = TQ + win
    dist = (win - 1) - np.arange(span)
    idx = np.clip(dist, -B_REL_CLIP, B_REL_CLIP) + B_REL_CLIP
    v = rel_bias.astype(F32)[:, idx] * LOG2E
    skew = jnp.tile(v, (1, TQ + 1))[:, :TQ * (span + 1)].reshape(N_HEADS, TQ, span + 1)
    tab = skew[:, ::-1, :win]
    t = np.arange(TQ)[:, None]
    jk = np.arange(win)[None, :]
    cdiff = (t // CHUNK + (B_WIN - 1) * TQ // CHUNK) - jk // CHUNK
    ok = (cdiff >= 0) & (cdiff <= B_PAST_CHUNKS)
    return jnp.where(jnp.asarray(ok)[None], tab, NEG)


def _mixer_d_kernel(q_ref, k0_ref, k1_ref, v0_ref, v1_ref, sink_ref, o_ref):
    i = pl.program_id(1)
    k_refs = (k0_ref, k1_ref)
    v_refs = (v0_ref, v1_ref)
    qpos = TQ + lax.broadcasted_iota(jnp.int32, (TQ, TQ), 0)
    bias = []
    for a in range(2):
        kpos = a * TQ + lax.broadcasted_iota(jnp.int32, (TQ, TQ), 1)
        cdiff = qpos // CHUNK - kpos // CHUNK
        ok = (cdiff >= 0) & (cdiff <= D_WINDOW_CHUNKS)
        if a == 0:
            ok = ok & (i > 0)
        bias.append(jnp.where(ok, 0.0, NEG))
    group = N_HEADS // D_KV_HEADS

    def piece(h):
        n = (h // group) * 2 + h % 2
        return slice(n * LANES, (n + 1) * LANES)

    _window_heads(q_ref, o_ref, 2,
                  lambda h, a: k_refs[a][:, piece(h)],
                  lambda h, a: v_refs[a][:, piece(h)],
                  lambda h, a: bias[a],
                  sink_of=lambda h: sink_ref[h:h + 1, 0:1])


def _mixer_d(q, k, v, sinks):
    bsz, seq, _ = q.shape
    qmap = lambda b, i: (b, i, 0)
    prev = lambda b, i: (b, jnp.maximum(i - 1, 0), 0)
    sink_tab = jnp.broadcast_to((sinks.astype(F32) * LOG2E)[:, None], (N_HEADS, LANES))
    return pl.pallas_call(
        _mixer_d_kernel,
        grid=(bsz, seq // TQ),
        in_specs=[pl.BlockSpec((None, TQ, HW), qmap),
                  pl.BlockSpec((None, TQ, HW), prev), pl.BlockSpec((None, TQ, HW), qmap),
                  pl.BlockSpec((None, TQ, HW), prev), pl.BlockSpec((None, TQ, HW), qmap),
                  pl.BlockSpec((N_HEADS, LANES), lambda b, i: (0, 0))],
        out_specs=pl.BlockSpec((None, TQ, HW), qmap),
        out_shape=jax.ShapeDtypeStruct((bsz, seq, HW), BF16),
        compiler_params=_cparams(2),
        name="mixer_d",
    )(q, k, k, v, v, sink_tab)


def _out_proj_kernel(h_ref, y0_ref, y1_ref, w0_ref, w1_ref, o_ref):
    o_ref[...] = h_ref[...] + _dot(y0_ref[...], w0_ref[...]) + _dot(y1_ref[...], w1_ref[...])


def _out_proj(h2, y0, y1, w0, w1, seq):
    m = h2.shape[0]
    tm = min(TM_PROJ, seq)
    row = lambda i: (i, 0)
    const = lambda i: (0, 0)
    return pl.pallas_call(
        _out_proj_kernel,
        grid=(m // tm,),
        in_specs=[pl.BlockSpec((tm, D_MODEL), row),
                  pl.BlockSpec((tm, HW), row), pl.BlockSpec((tm, HW), row),
                  pl.BlockSpec((HW, D_MODEL), const), pl.BlockSpec((HW, D_MODEL), const)],
        out_specs=pl.BlockSpec((tm, D_MODEL), row),
        out_shape=jax.ShapeDtypeStruct((m, D_MODEL), F32),
        compiler_params=_cparams(1),
        name="out_proj",
    )(h2, y0, y1, w0, w1)


def _ffn_kernel(h_ref, halo_ref, g_ref, wg_ref, wu_ref, cwg_ref, cwu_ref, cbg_ref, cbu_ref,
                wd_ref, go_ref, o_ref, xn_ref, xh_ref, acc_ref, hb_ref, *, nseq, final_norm):
    i = pl.program_id(0)
    f = pl.program_id(1)
    tm = h_ref.shape[0]

    @pl.when(f == 0)
    def _():
        xn_ref[...] = _rms(h_ref[...], g_ref[...]).astype(BF16)
        xh_ref[...] = _rms(halo_ref[...], g_ref[...]).astype(BF16)
        acc_ref[...] = jnp.zeros_like(acc_ref)

    halo_on = jnp.where(i % nseq == 0, 0.0, 1.0)

    def branch(w_ref, cw_ref, cb_ref):
        hb_ref[HALO:, :] = _dot(xn_ref[...], w_ref[...])
        hb_ref[:HALO, :] = _dot(xh_ref[...], w_ref[...]) * halo_on
        cw = cw_ref[...]
        return (hb_ref[pl.ds(HALO, tm), :] * cw[2:3, :]
                + hb_ref[pl.ds(HALO - 1, tm), :] * cw[1:2, :]
                + hb_ref[pl.ds(HALO - 2, tm), :] * cw[0:1, :]
                + cb_ref[...])

    yg = branch(wg_ref, cwg_ref, cbg_ref)
    yu = branch(wu_ref, cwu_ref, cbu_ref)
    act = (yg / (1.0 + jnp.exp(-yg))) * yu
    acc_ref[...] += _dot(act.astype(BF16), wd_ref[...])

    @pl.when(f == pl.num_programs(1) - 1)
    def _():
        out = h_ref[...] + acc_ref[...]
        if final_norm:
            out = _rms(out, go_ref[...])
        o_ref[...] = out


def _ffn(h2, g, w_up, conv_w, conv_b, w_down, g_out, seq, final_norm):
    m = h2.shape[0]
    tm = min(TM_PROJ, seq)
    nseq = seq // tm
    nf = D_FF // TF
    row = lambda i, f: (i, 0)
    const = lambda i, f: (0, 0)
    halo = lambda i, f: (jnp.maximum(i * (tm // HALO) - 1, 0), 0)
    gcol = lambda i, f: (0, f)
    ucol = lambda i, f: (0, nf + f)
    return pl.pallas_call(
        functools.partial(_ffn_kernel, nseq=nseq, final_norm=final_norm),
        grid=(m // tm, nf),
        in_specs=[pl.BlockSpec((tm, D_MODEL), row),
                  pl.BlockSpec((HALO, D_MODEL), halo),
                  pl.BlockSpec((1, D_MODEL), const),
                  pl.BlockSpec((D_MODEL, TF), gcol), pl.BlockSpec((D_MODEL, TF), ucol),
                  pl.BlockSpec((3, TF), gcol), pl.BlockSpec((3, TF), ucol),
                  pl.BlockSpec((1, TF), gcol), pl.BlockSpec((1, TF), ucol),
                  pl.BlockSpec((TF, D_MODEL), lambda i, f: (f, 0)),
                  pl.BlockSpec((1, D_MODEL), const)],
        out_specs=pl.BlockSpec((tm, D_MODEL), row),
        out_shape=jax.ShapeDtypeStruct((m, D_MODEL), F32),
        scratch_shapes=[pltpu.VMEM((tm, D_MODEL), BF16),
                        pltpu.VMEM((HALO, D_MODEL), BF16),
                        pltpu.VMEM((tm, D_MODEL), F32),
                        pltpu.VMEM((tm + HALO, TF), F32)],
        compiler_params=_cparams(2),
        name="ffn",
    )(h2, h2, g, w_up, w_up, conv_w, conv_w, conv_b, conv_b, w_down, g_out)


def _rope_tables(seq):
    inv = 1.0 / (ROPE_THETA ** (jnp.arange(0, HEAD_DIM, 2, dtype=F32) / HEAD_DIM))
    ang = jnp.arange(seq, dtype=F32)[:, None] * inv[None, :]
    c, s = jnp.cos(ang), jnp.sin(ang)
    cos = jnp.concatenate([c, c, c, c], axis=1)
    sin = jnp.concatenate([-s, s, -s, s], axis=1)
    return cos, sin


def _pad_cols(w, width):
    return jnp.pad(w, ((0, 0), (0, width - w.shape[1])))


def _even_weights(w_in):
    o = np.cumsum([0, 512, 512, 512, IDX_HEADS * IDX_DIM, IDX_DIM, IDX_HEADS, 512, 512, 512])
    sl = [w_in[:, o[n]:o[n + 1]] for n in range(9)]
    small = jnp.concatenate([_pad_cols(sl[4], LANES), _pad_cols(sl[5], LANES)], axis=1)
    return jnp.concatenate([sl[0], sl[1], sl[2], sl[3], small, sl[6], sl[7], sl[8]],
                           axis=1).astype(BF16)


def _odd_weights(w_in):
    o = np.cumsum([0, 512, 512, 512, N_HEADS, 512, D_KV_HEADS * HEAD_DIM, D_KV_HEADS * HEAD_DIM])
    sl = [w_in[:, o[n]:o[n + 1]] for n in range(7)]
    return jnp.concatenate([sl[0], sl[1], sl[2], _pad_cols(sl[3], LANES), sl[4], sl[5], sl[6]],
                           axis=1).astype(BF16)


def _row(v, width=None):
    v = v.astype(F32)[None, :]
    return v if width is None else _pad_cols(v, width)


def kernel(x, norm_mix_g, norm_ffn_g, norm_out_g, even_w_in, even_w_out, idx_k_ln_g, idx_k_ln_b,
           rel_bias, odd_w_in, odd_w_out, forget_b, sinks, ffn_w_up, ffn_conv_w, ffn_conv_b,
           ffn_w_down):
    bsz, seq, _ = x.shape
    m = bsz * seq
    depth = norm_mix_g.shape[0]
    assert seq % TQ == 0 and seq % min(TM_PROJ, seq) == 0 and seq % min(TKB_MAX, seq) == 0
    cos, sin = _rope_tables(seq)
    h = x.reshape(m, D_MODEL)
    as3 = lambda t: t.reshape(bsz, seq, t.shape[-1])

    for layer in range(depth):
        jj = layer // 2
        g_mix = _row(norm_mix_g[layer])
        if layer % 2 == 0:
            qa, ka, va, qi, ki, wi, qb, kb, vb = _even_proj(
                h, g_mix, _even_weights(even_w_in[jj]), cos, sin,
                _row(idx_k_ln_g[jj], LANES), _row(idx_k_ln_b[jj], LANES), seq)
            kit = jnp.swapaxes(as3(ki), 1, 2)
            kie, kio = kit, jnp.roll(kit, IDX_DIM, axis=1)
            y0 = _mixer_a(as3(qi), as3(wi), kie, kio, as3(qa), as3(ka), as3(va))
            y1 = _mixer_b(as3(qb), as3(kb), as3(vb), _b_bias_table(rel_bias[jj]))
            w_out = even_w_out[jj]
        else:
            qc, kc, vc, fcum, qd, kd, vd = _odd_proj(
                h, g_mix, _odd_weights(odd_w_in[jj]), cos, sin,
                _row(forget_b[jj], LANES), seq)
            fq = as3(fcum)
            fk = jnp.swapaxes(fq[:, :, :N_HEADS], 1, 2)
            y0 = _mixer_c(as3(qc), as3(kc), as3(vc), fq, fk)
            y1 = _mixer_d(as3(qd), as3(kd), as3(vd), sinks[jj])
            w_out = odd_w_out[jj]
        w_out = w_out.astype(BF16)
        h = _out_proj(h, y0.reshape(m, HW), y1.reshape(m, HW), w_out[:HW], w_out[HW:], seq)
        h = _ffn(h, _row(norm_ffn_g[layer]), ffn_w_up[layer].astype(BF16),
                 ffn_conv_w[layer].astype(F32), _row(ffn_conv_b[layer]),
                 ffn_w_down[layer].astype(BF16), _row(norm_out_g), seq,
                 final_norm=(layer == depth - 1))
    return h.reshape(bsz, seq, D_MODEL)
```

```python
import functools

import numpy as np
import jax
import jax.numpy as jnp
from jax import lax
from jax.experimental import pallas as pl
from jax.experimental.pallas import tpu as pltpu

F32 = jnp.float32
BF16 = jnp.bfloat16

D_MODEL = 1024
HEAD_DIM = 64
CHUNK = 64
ROPE_THETA = 10000.0
EPS = 1e-6
NEG = -1e30
N_HEADS = 8
IDX_HEADS = 4
IDX_DIM = 64
IDX_TOPK_MAX = 256
B_PAST_CHUNKS = 8
B_REL_CLIP = 256
D_KV_HEADS = 2
D_WINDOW_CHUNKS = 2
D_FF = 2816
HW = N_HEADS * HEAD_DIM
SCALE = HEAD_DIM ** -0.5
LOG2E = 1.4426950408889634
QSCALE = SCALE * LOG2E

LANES = 128
VMEM_LIMIT = 56 * 1024 * 1024

TM_PROJ = 512
TM_FFN = 1024
TQ = 256
TQ_C = 256
SUB = 512
TKB_MAX = 2048
TF = 256
HALO = 16
LAZY_MARGIN = 20.0


def _cparams(n_axes):
    return pltpu.CompilerParams(
        dimension_semantics=("arbitrary",) * n_axes,
        vmem_limit_bytes=VMEM_LIMIT)


def _rms(x, g):
    ms = jnp.mean(x * x, axis=-1, keepdims=True)
    return x * lax.rsqrt(ms + EPS) * g


def _lane_iota(shape):
    return lax.broadcasted_iota(jnp.int32, shape, len(shape) - 1)


def _rope(t, cos, sin):
    w = t.shape[1]
    lane = _lane_iota(t.shape)
    fwd = pltpu.roll(t, 32, 1)
    bwd = pltpu.roll(t, w - 32, 1)
    rot = jnp.where((lane % HEAD_DIM) < (HEAD_DIM // 2), bwd, fwd)
    return t * cos + rot * sin


def _tile_lanes(t, reps):
    return t if reps == 1 else jnp.concatenate([t] * reps, axis=1)


def _head_pieces(p, sw=None):
    lane = _lane_iota(p.shape)
    lo = lane < HEAD_DIM
    q = p if sw is None else sw
    even = jnp.where(lane == HEAD_DIM, 1.0, jnp.where(lo, p, 0.0))
    odd = jnp.where(lane == 0, 1.0, jnp.where(lo, 0.0, q))
    return even, odd


def _expand_heads(t, ones_lane=False):
    pieces = []
    for j in range(t.shape[1] // LANES):
        p = t[:, j * LANES:(j + 1) * LANES]
        if ones_lane:
            pieces.extend(_head_pieces(p))
        else:
            lo = _lane_iota(p.shape) < HEAD_DIM
            pieces.append(jnp.where(lo, p, 0.0))
            pieces.append(jnp.where(lo, 0.0, p))
    return jnp.concatenate(pieces, axis=1).astype(BF16)


def _expand_kv_pair(t, ones_lane=False):
    lo = _lane_iota(t.shape) < HEAD_DIM
    sw = pltpu.roll(t, HEAD_DIM, 1)
    if ones_lane:
        e0, o0 = _head_pieces(t, sw)
        e1, o1 = _head_pieces(sw, t)
        pieces = [e0, o0, e1, o1]
    else:
        pieces = [jnp.where(lo, t, 0.0), jnp.where(lo, 0.0, sw),
                  jnp.where(lo, sw, 0.0), jnp.where(lo, 0.0, t)]
    return jnp.concatenate(pieces, axis=1).astype(BF16)


def _dot(a, b):
    return jnp.dot(a, b, preferred_element_type=F32)


def _dot_nt(a, b):
    return lax.dot_general(a, b, (((1,), (1,)), ((), ())), preferred_element_type=F32)


def _even_proj_kernel(x_ref, g_ref, w_ref, cos_ref, sin_ref, lng_ref, lnb_ref,
                      qa_ref, ka_ref, va_ref, qi_ref, ki_ref, wi_ref,
                      qb_ref, kb_ref, vb_ref):
    xn = _rms(x_ref[...], g_ref[...]).astype(BF16)
    cos = cos_ref[...]
    sin = sin_ref[...]
    cos4 = _tile_lanes(cos, 4)
    sin4 = _tile_lanes(sin, 4)

    def proj(a, b):
        return _dot(xn, w_ref[:, a:b])

    qa_ref[...] = (_rope(proj(0, 512), cos4, sin4) * QSCALE).astype(BF16)
    ka_ref[...] = _expand_heads(_rope(proj(512, 1024), cos4, sin4))
    va_ref[...] = _expand_heads(proj(1024, 1536), ones_lane=True)
    qi_ref[...] = _rope(proj(1536, 1792), _tile_lanes(cos, 2), _tile_lanes(sin, 2)).astype(BF16)
    small = proj(1792, 2048)
    kraw = small[:, :LANES]
    wi_ref[...] = small[:, LANES:]
    valid = _lane_iota(kraw.shape) < IDX_DIM
    mean = jnp.sum(kraw, axis=-1, keepdims=True) * (1.0 / IDX_DIM)
    xc = jnp.where(valid, kraw - mean, 0.0)
    var = jnp.sum(xc * xc, axis=-1, keepdims=True) * (1.0 / IDX_DIM)
    kn = xc * lax.rsqrt(var + EPS) * lng_ref[...] + lnb_ref[...]
    ki_ref[...] = _rope(kn, cos, sin).astype(BF16)
    qb_ref[...] = (proj(2048, 2560) * QSCALE).astype(BF16)
    kb_ref[...] = _expand_heads(proj(2560, 3072))
    vb_ref[...] = _expand_heads(proj(3072, 3584), ones_lane=True)


def _even_proj(x2, g, w, cos, sin, lng, lnb, seq):
    m = x2.shape[0]
    tm = min(TM_PROJ, seq)
    nseq = seq // tm
    row = lambda i: (i, 0)
    const = lambda i: (0, 0)
    tab = lambda i: (i % nseq, 0)
    widths = [(512, BF16), (1024, BF16), (1024, BF16), (256, BF16), (128, BF16),
              (128, F32), (512, BF16), (1024, BF16), (1024, BF16)]
    return pl.pallas_call(
        _even_proj_kernel,
        grid=(m // tm,),
        in_specs=[pl.BlockSpec((tm, D_MODEL), row),
                  pl.BlockSpec((1, D_MODEL), const),
                  pl.BlockSpec(w.shape, const),
                  pl.BlockSpec((tm, LANES), tab),
                  pl.BlockSpec((tm, LANES), tab),
                  pl.BlockSpec((1, LANES), const),
                  pl.BlockSpec((1, LANES), const)],
        out_specs=[pl.BlockSpec((tm, wd), row) for wd, _ in widths],
        out_shape=[jax.ShapeDtypeStruct((m, wd), dt) for wd, dt in widths],
        compiler_params=_cparams(1),
        name="even_proj",
    )(x2, g, w, cos, sin, lng, lnb)


def _odd_proj_kernel(x_ref, g_ref, w_ref, cos_ref, sin_ref, fb_ref, tril_ref,
                     qc_ref, kc_ref, vc_ref, f_ref, qd_ref, kd_ref, vd_ref,
                     carry_ref, *, nseq):
    i = pl.program_id(0)
    xn = _rms(x_ref[...], g_ref[...]).astype(BF16)
    cos = cos_ref[...]
    sin = sin_ref[...]

    def proj(a, b):
        return _dot(xn, w_ref[:, a:b])

    qc_ref[...] = (proj(0, 512) * QSCALE).astype(BF16)
    kc_ref[...] = _expand_heads(proj(512, 1024))
    vc_ref[...] = _expand_heads(proj(1024, 1536), ones_lane=True)

    z = proj(1536, 1664) + fb_ref[...]
    lf = jnp.minimum(z, 0.0) - jnp.log(1.0 + jnp.exp(-jnp.abs(z)))
    lf = jnp.where(_lane_iota(lf.shape) < N_HEADS, lf, 0.0)
    hi = lf.astype(BF16)
    r1 = lf - hi.astype(F32)
    mid = r1.astype(BF16)
    lo = (r1 - mid.astype(F32)).astype(BF16)
    tril = tril_ref[...]
    csum = _dot(tril, hi) + _dot(tril, mid) + _dot(tril, lo)

    @pl.when(i % nseq == 0)
    def _():
        carry_ref[...] = jnp.zeros_like(carry_ref)

    fcum = csum + carry_ref[0:1, :]
    f_ref[...] = fcum * LOG2E
    carry_ref[...] = jnp.broadcast_to(fcum[fcum.shape[0] - 1:, :], carry_ref.shape)

    qd_ref[...] = (_rope(proj(1664, 2176), _tile_lanes(cos, 4), _tile_lanes(sin, 4))
                   * QSCALE).astype(BF16)
    kv = proj(2176, 2432)
    kd_ref[...] = _expand_kv_pair(_rope(kv[:, :LANES], cos, sin))
    vd_ref[...] = _expand_kv_pair(kv[:, LANES:], ones_lane=True)


def _odd_proj(x2, g, w, cos, sin, fb, seq):
    m = x2.shape[0]
    tm = min(TM_PROJ, seq)
    nseq = seq // tm
    row = lambda i: (i, 0)
    const = lambda i: (0, 0)
    tab = lambda i: (i % nseq, 0)
    tril = jnp.tril(jnp.ones((tm, tm), F32)).astype(BF16)
    widths = [(512, BF16), (1024, BF16), (1024, BF16), (128, F32),
              (512, BF16), (512, BF16), (512, BF16)]
    return pl.pallas_call(
        functools.partial(_odd_proj_kernel, nseq=nseq),
        grid=(m // tm,),
        in_specs=[pl.BlockSpec((tm, D_MODEL), row),
                  pl.BlockSpec((1, D_MODEL), const),
                  pl.BlockSpec(w.shape, const),
                  pl.BlockSpec((tm, LANES), tab),
                  pl.BlockSpec((tm, LANES), tab),
                  pl.BlockSpec((1, LANES), const),
                  pl.BlockSpec((tm, tm), const)],
        out_specs=[pl.BlockSpec((tm, wd), row) for wd, _ in widths],
        out_shape=[jax.ShapeDtypeStruct((m, wd), dt) for wd, dt in widths],
        scratch_shapes=[pltpu.VMEM((8, LANES), F32)],
        compiler_params=_cparams(1),
        name="odd_proj",
    )(x2, g, w, cos, sin, fb, tril)


def _denominator_lane(h):
    return HEAD_DIM if h % 2 == 0 else 0


def _flash_heads(q_ref, k_ref, v_ref, rows, acc_ref, m_ref, logit_fn):
    nrep = SUB // LANES
    s_next = _head_scores(q_ref, k_ref, rows, 0)
    for h in range(N_HEADS):
        s_raw = s_next
        if h + 1 < N_HEADS:
            s_next = _head_scores(q_ref, k_ref, rows, h + 1)
        s, shift = logit_fn(h, s_raw)
        m_prev = m_ref[h]
        m_cur = jnp.max(s, axis=1, keepdims=True)
        if shift is not None:
            m_cur = m_cur + shift
        m_new = jnp.maximum(m_prev, m_cur)
        alpha = jnp.exp2(m_prev - m_new)
        off = m_new if shift is None else m_new - shift
        p = jnp.exp2(s - _tile_lanes(off, nrep))
        m_ref[h] = m_new
        cols = slice(h * LANES, (h + 1) * LANES)
        acc_ref[:, cols] = acc_ref[:, cols] * alpha + _dot(p.astype(BF16), v_ref[rows, cols])


def _head_scores(q_ref, k_ref, rows, h):
    jp = h // 2
    return _dot_nt(q_ref[:, jp * LANES:(jp + 1) * LANES], k_ref[rows, h * LANES:(h + 1) * LANES])


def _flash_heads_lazy(q_ref, k_ref, v_ref, rows, acc_ref, m_ref, tmp_ref, logit_fn):
    nrep = SUB // LANES
    worst = None
    pending = [_head_scores(q_ref, k_ref, rows, h) for h in range(2)]
    for h in range(N_HEADS):
        if h + 2 < N_HEADS:
            pending.append(_head_scores(q_ref, k_ref, rows, h + 2))
        s, shift = logit_fn(h, pending.pop(0))
        ref = m_ref[h] if shift is None else m_ref[h] - shift
        lane_max = functools.reduce(
            jnp.maximum, [s[:, t * LANES:(t + 1) * LANES] for t in range(nrep)])
        over = lane_max - ref
        worst = over if worst is None else jnp.maximum(worst, over)
        p = jnp.exp2(s - _tile_lanes(ref, nrep))
        cols = slice(h * LANES, (h + 1) * LANES)
        tmp_ref[:, cols] = _dot(p.astype(BF16), v_ref[rows, cols])
    exceeded = jnp.max(worst) > LAZY_MARGIN

    @pl.when(jnp.logical_not(exceeded))
    def _():
        acc_ref[...] += tmp_ref[...]

    @pl.when(exceeded)
    def _():
        _flash_heads(q_ref, k_ref, v_ref, rows, acc_ref, m_ref, logit_fn)


def _flash_init(acc_ref, m_ref):
    acc_ref[...] = jnp.zeros_like(acc_ref)
    m_ref[...] = jnp.full(m_ref.shape, NEG, F32)


def _pair_output(even, odd, l_even, l_odd):
    lo = _lane_iota(even.shape) < HEAD_DIM
    return jnp.where(lo, even / l_even, odd / l_odd)


def _flash_finish(o_ref, acc_ref):
    for jp in range(N_HEADS // 2):
        even = acc_ref[:, (2 * jp) * LANES:(2 * jp + 1) * LANES]
        odd = acc_ref[:, (2 * jp + 1) * LANES:(2 * jp + 2) * LANES]
        le = even[:, _denominator_lane(0):_denominator_lane(0) + 1]
        lo_ = odd[:, _denominator_lane(1):_denominator_lane(1) + 1]
        o_ref[:, jp * LANES:(jp + 1) * LANES] = _pair_output(even, odd, le, lo_).astype(o_ref.dtype)


def _kv_tiles(seq):
    tkb = min(TKB_MAX, seq)
    return tkb, seq // tkb, tkb // SUB


def _mixer_c_kernel(q_ref, k_ref, v_ref, fq_ref, fk_ref, o_ref, acc_ref, m_ref, fqr_ref,
                    *, tq, tkb, nsub):
    i = pl.program_id(1)
    j = pl.program_id(2)
    q_lo = i * tq
    last_sub = (q_lo + tq - 1) // SUB
    first_masked = q_lo // SUB
    jmax = last_sub // nsub

    @pl.when(j == 0)
    def _():
        _flash_init(acc_ref, m_ref)
        for h in range(N_HEADS):
            fqr_ref[h] = jnp.broadcast_to(fq_ref[:, h:h + 1], (tq, LANES))

    def step(c, masked):
        rows = pl.ds(pl.multiple_of(c * SUB, SUB), SUB)

        def logit_fn(h, s):
            s = s - fk_ref[h:h + 1, rows]
            if masked:
                kpos = j * tkb + c * SUB + lax.broadcasted_iota(jnp.int32, s.shape, 1)
                qpos = q_lo + lax.broadcasted_iota(jnp.int32, s.shape, 0)
                s = jnp.where(kpos <= qpos, s, NEG)
            return s, fqr_ref[h]

        _flash_heads(q_ref, k_ref, v_ref, rows, acc_ref, m_ref, logit_fn)

    @pl.when(j <= jmax)
    def _():
        lo = j * nsub
        n_full = jnp.clip(first_masked - lo, 0, nsub)
        n_all = jnp.clip(last_sub + 1 - lo, 0, nsub)

        def body(c, carry):
            step(c, False)
            return carry

        def masked_body(c, carry):
            step(c, True)
            return carry

        lax.fori_loop(0, n_full, body, 0)
        lax.fori_loop(n_full, n_all, masked_body, 0)

        @pl.when(j == jmax)
        def _():
            _flash_finish(o_ref, acc_ref)


def _mixer_c(q, k, v, fq, fk):
    bsz, seq, _ = q.shape
    tq = min(TQ_C, seq)
    tkb, nkb, nsub = _kv_tiles(seq)
    kv_idx = lambda b, i, j: (b, jnp.minimum(j, (i * tq + tq - 1) // tkb), 0)
    return pl.pallas_call(
        functools.partial(_mixer_c_kernel, tq=tq, tkb=tkb, nsub=nsub),
        grid=(bsz, seq // tq, nkb),
        in_specs=[pl.BlockSpec((None, tq, HW), lambda b, i, j: (b, i, 0)),
                  pl.BlockSpec((None, tkb, 2 * HW), kv_idx),
                  pl.BlockSpec((None, tkb, 2 * HW), kv_idx),
                  pl.BlockSpec((None, tq, LANES), lambda b, i, j: (b, i, 0)),
                  pl.BlockSpec((None, N_HEADS, tkb),
                               lambda b, i, j: (b, 0, jnp.minimum(j, (i * tq + tq - 1) // tkb)))],
        out_specs=pl.BlockSpec((None, tq, HW), lambda b, i, j: (b, i, 0)),
        out_shape=jax.ShapeDtypeStruct((bsz, seq, HW), BF16),
        scratch_shapes=[pltpu.VMEM((tq, 2 * HW), F32),
                        pltpu.VMEM((N_HEADS, tq, LANES), F32),
                        pltpu.VMEM((N_HEADS, tq, LANES), F32)],
        compiler_params=_cparams(3),
        name="mixer_c",
    )(q, k, v, fq, fk)


MIN_NORMAL_KEY = 0x00800000


def _key_flip(bits):
    return jnp.where(bits >= 0, bits, bits ^ jnp.int32(0x7FFFFFFF))


def _float_to_key(f):
    return _key_flip(pltpu.bitcast(f, jnp.int32))


def _key_to_float(k):
    return pltpu.bitcast(_key_flip(k), F32)


def _mixer_a_kernel(qi_ref, wi_ref, kie_ref, kio_ref, q_ref, k_ref, v_ref, tri_ref, zero_ref,
                    o_ref, s_ref, gmax_ref, cnt_ref, acc_ref, m_ref, tmp_ref,
                    *, tkb, nsub, topk):
    i = pl.program_id(1)
    j = pl.program_id(2)
    q_lo = i * TQ
    last_sub = q_lo // SUB
    jmax = last_sub // nsub
    n_sub = last_sub + 1
    nrep = SUB // LANES
    half = TQ // 2

    def admissible(c_glob, shape):
        kpos = c_glob * SUB + lax.broadcasted_iota(jnp.int32, shape, 1)
        qpos = q_lo + lax.broadcasted_iota(jnp.int32, shape, 0)
        return (kpos // CHUNK) <= (qpos // CHUNK)

    def s_load(c):
        return jnp.concatenate([s_ref[c * nrep + t] for t in range(nrep)], axis=1)

    def s_store(c, val):
        for t in range(nrep):
            s_ref[c * nrep + t] = val[:, t * LANES:(t + 1) * LANES]

    @pl.when(j == 0)
    def _():
        _flash_init(acc_ref, m_ref)
        cnt_ref[...] = jnp.zeros_like(cnt_ref)
        gmax_ref[...] = jnp.full(gmax_ref.shape, -jnp.inf, F32)

        def score(c, masked):
            cols = pl.ds(pl.multiple_of(c * SUB, SUB), SUB)
            tot = None
            for pair in range(IDX_HEADS // 2):
                qp = qi_ref[:, pair * LANES:(pair + 1) * LANES]
                for e in range(2):
                    h = 2 * pair + e
                    kt = (kie_ref if e == 0 else kio_ref)[:, cols]
                    term = jnp.maximum(_dot(qp, kt), 0.0) * wi_ref[:, h:h + 1]
                    tot = term if tot is None else tot + term
            if masked:
                tot = jnp.where(admissible(c, tot.shape), tot, NEG)
            s_store(c, tot)
            for par in range(2):
                g = gmax_ref[par]
                for t in range(par, nrep, 2):
                    g = jnp.maximum(g, tot[:, t * LANES:(t + 1) * LANES])
                gmax_ref[par] = g

        def score_body(c, carry):
            score(c, False)
            return carry

        lax.fori_loop(0, last_sub, score_body, 0)
        score(last_sub, True)

        def count(cand, strict):
            outs = []
            for hh in range(2):
                r0 = hh * half
                ch = cand[r0:r0 + half]

                def body(c, acc):
                    for t in range(nrep):
                        blk = s_ref[c * nrep + t, r0:r0 + half, :]
                        hit = (blk > ch) if strict else (blk >= ch)
                        acc = acc + jnp.where(hit, 1.0, 0.0)
                    return acc

                acc = lax.fori_loop(0, n_sub, body, jnp.zeros((half, LANES), F32))
                outs.append(jnp.broadcast_to(jnp.sum(acc, axis=1, keepdims=True), (half, LANES)))
            return jnp.concatenate(outs, axis=0)

        ga = gmax_ref[0]
        gb = gmax_ref[1]
        lo_f = jnp.min(jnp.minimum(ga, gb), axis=1, keepdims=True)
        hi_f = jnp.max(jnp.maximum(ga, gb), axis=1, keepdims=True)
        lo0 = jnp.broadcast_to(_float_to_key(lo_f), (TQ, LANES))
        hi0 = jnp.broadcast_to(_float_to_key(hi_f), (TQ, LANES))

        def unresolved(st):
            lo, hi = st
            return jnp.max(jnp.where(lo < hi, 1.0, 0.0)) > 0.5

        def narrow(lo, hi, mid):
            n_ge = count(_key_to_float(mid), False)
            ge = n_ge >= topk
            below = jnp.where(mid == MIN_NORMAL_KEY, 0, mid - 1)
            return (jnp.where(ge, mid, lo),
                    jnp.where(n_ge == topk, mid, jnp.where(ge, hi, below)))

        def bisect(st):
            lo, hi = st
            by_key = (lo >> 1) + (hi >> 1) + ((lo | hi) & 1)
            lo_f = _key_to_float(lo)
            hi_f = _key_to_float(hi)
            by_val = _float_to_key(lo_f + (hi_f - lo_f) * 0.5)
            by_val = jnp.minimum(jnp.maximum(by_val, lo + 1), hi)
            same_binade = (pltpu.bitcast(lo_f, jnp.int32) >> 23) == (pltpu.bitcast(hi_f, jnp.int32) >> 23)
            mid = jnp.where(same_binade, by_key, by_val)
            mid = jnp.where((lo < 0) & (hi >= 0), 0, mid)
            mid = jnp.where((lo == 0) & (hi >= MIN_NORMAL_KEY), MIN_NORMAL_KEY, mid)
            mid = jnp.where(lo < hi, mid, lo)
            return narrow(lo, hi, mid)

        lo, _ = lax.while_loop(unresolved, bisect, narrow(lo0, hi0, hi0))
        tau1 = _key_to_float(lo)
        keep1 = topk - count(tau1, True)
        tau = _tile_lanes(tau1, nrep)
        keep = _tile_lanes(keep1, nrep)

        def to_mask(c, masked):
            blk = s_load(c)
            eq = blk == tau
            rank = (_tile_lanes(cnt_ref[...], nrep)
                    + _dot(jnp.where(eq, 1.0, 0.0).astype(BF16), tri_ref[...]))
            keep_tie = jnp.where(rank <= keep, 0.0, NEG)
            bias = jnp.where(blk > tau, 0.0, jnp.where(eq, keep_tie, NEG))
            if masked:
                bias = jnp.where(admissible(c, bias.shape), bias, NEG)
            cnt_ref[...] = jnp.broadcast_to(rank[:, SUB - 1:SUB], (TQ, LANES))
            s_store(c, bias)

        def mask_body(c, carry):
            to_mask(c, False)
            return carry

        lax.fori_loop(0, last_sub, mask_body, 0)
        to_mask(last_sub, True)

    @pl.when(j <= jmax)
    def _():
        def body(c, carry):
            rows = pl.ds(pl.multiple_of(c * SUB, SUB), SUB)
            _flash_heads_lazy(q_ref, k_ref, v_ref, rows, acc_ref, m_ref, tmp_ref,
                              lambda h, s: (s + s_load(j * nsub + c + zero_ref[h]), None))
            return carry

        lax.fori_loop(0, jnp.minimum(n_sub - j * nsub, nsub), body, 0)

        @pl.when(j == jmax)
        def _():
            _flash_finish(o_ref, acc_ref)


def _mixer_a(qi, wi, kie, kio, q, k, v):
    bsz, seq, _ = q.shape
    tkb, nkb, nsub = _kv_tiles(seq)
    topk = min(IDX_TOPK_MAX, seq // 4)
    tri = jnp.triu(jnp.ones((SUB, SUB), F32)).astype(BF16)
    qmap = lambda b, i, j: (b, i, 0)
    kv_idx = lambda b, i, j: (b, jnp.minimum(j, (i * TQ) // tkb), 0)
    res = lambda b, i, j: (b, 0, 0)
    return pl.pallas_call(
        functools.partial(_mixer_a_kernel, tkb=tkb, nsub=nsub, topk=topk),
        grid=(bsz, seq // TQ, nkb),
        in_specs=[pl.BlockSpec((None, TQ, IDX_HEADS * IDX_DIM), qmap),
                  pl.BlockSpec((None, TQ, LANES), qmap),
                  pl.BlockSpec((None, LANES, seq), res),
                  pl.BlockSpec((None, LANES, seq), res),
                  pl.BlockSpec((None, TQ, HW), qmap),
                  pl.BlockSpec((None, tkb, 2 * HW), kv_idx),
                  pl.BlockSpec((None, tkb, 2 * HW), kv_idx),
                  pl.BlockSpec((SUB, SUB), lambda b, i, j: (0, 0)),
                  pl.BlockSpec(memory_space=pltpu.SMEM)],
        out_specs=pl.BlockSpec((None, TQ, HW), qmap),
        out_shape=jax.ShapeDtypeStruct((bsz, seq, HW), BF16),
        scratch_shapes=[pltpu.VMEM((seq // LANES, TQ, LANES), F32),
                        pltpu.VMEM((2, TQ, LANES), F32),
                        pltpu.VMEM((TQ, LANES), F32),
                        pltpu.VMEM((TQ, 2 * HW), F32),
                        pltpu.VMEM((N_HEADS, TQ, LANES), F32),
                        pltpu.VMEM((TQ, 2 * HW), F32)],
        compiler_params=_cparams(3),
        name="mixer_a",
    )(qi, wi, kie, kio, q, k, v, tri, jnp.zeros((N_HEADS,), jnp.int32))


B_WIN = 3


def _window_heads(q_ref, o_ref, n_win, k_of, v_of, bias_of, sink_of=None):
    def scores(h):
        jp = h // 2
        qp = q_ref[:, jp * LANES:(jp + 1) * LANES]
        return [_dot_nt(qp, k_of(h, a)) + bias_of(h, a) for a in range(n_win)]

    s_next = scores(0)
    done = []
    for h in range(N_HEADS):
        s = s_next
        if h + 1 < N_HEADS:
            s_next = scores(h + 1)
        m = functools.reduce(jnp.maximum, [jnp.max(t, axis=1, keepdims=True) for t in s])
        if sink_of is not None:
            m = jnp.maximum(m, sink_of(h))
        pv = functools.reduce(
            jnp.add, [_dot(jnp.exp2(s[a] - m).astype(BF16), v_of(h, a)) for a in range(n_win)])
        lane = _denominator_lane(h)
        l = pv[:, lane:lane + 1]
        if sink_of is not None:
            l = l + jnp.exp2(sink_of(h) - m)
        done.append((pv, l))
        if h % 2 == 1:
            (pe, le), (po, lo_) = done[-2], done[-1]
            jp = h // 2
            o_ref[:, jp * LANES:(jp + 1) * LANES] = _pair_output(pe, po, le, lo_).astype(o_ref.dtype)


def _mixer_b_kernel(q_ref, k0_ref, k1_ref, k2_ref, v0_ref, v1_ref, v2_ref, bias_ref, o_ref):
    i = pl.program_id(1)
    k_refs = (k0_ref, k1_ref, k2_ref)
    v_refs = (v0_ref, v1_ref, v2_ref)
    pad = [jnp.where(i - (B_WIN - 1 - a) >= 0, 0.0, NEG) for a in range(B_WIN)]
    hc = lambda h: slice(h * LANES, (h + 1) * LANES)
    _window_heads(q_ref, o_ref, B_WIN,
                  lambda h, a: k_refs[a][:, hc(h)],
                  lambda h, a: v_refs[a][:, hc(h)],
                  lambda h, a: bias_ref[h, :, a * TQ:(a + 1) * TQ] + pad[a])


def _mixer_b(q, k, v, bias_tab):
    bsz, seq, _ = q.shape
    qmap = lambda b, i: (b, i, 0)
    kmaps = [functools.partial(lambda b, i, d: (b, jnp.maximum(i - d, 0), 0), d=B_WIN - 1 - a)
             for a in range(B_WIN)]
    kspec = [pl.BlockSpec((None, TQ, 2 * HW), km) for km in kmaps]
    return pl.pallas_call(
        _mixer_b_kernel,
        grid=(bsz, seq // TQ),
        in_specs=[pl.BlockSpec((None, TQ, HW), qmap)] + kspec + kspec
                 + [pl.BlockSpec(bias_tab.shape, lambda b, i: (0, 0, 0))],
        out_specs=pl.BlockSpec((None, TQ, HW), qmap),
        out_shape=jax.ShapeDtypeStruct((bsz, seq, HW), BF16),
        compiler_params=_cparams(2),
        name="mixer_b",
    )(q, k, k, k, v, v, v, bias_tab)


def _b_bias_table(rel_bias):
    win = B_WIN * TQ
    span = TQ + win
    dist = (win - 1) - np.arange(span)
    idx = np.clip(dist, -B_REL_CLIP, B_REL_CLIP) + B_REL_CLIP
    v = rel_bias.astype(F32)[:, idx] * LOG2E
    skew = jnp.tile(v, (1, TQ + 1))[:, :TQ * (span + 1)].reshape(N_HEADS, TQ, span + 1)
    tab = skew[:, ::-1, :win]
    t = np.arange(TQ)[:, None]
    jk = np.arange(win)[None, :]
    cdiff = (t // CHUNK + (B_WIN - 1) * TQ // CHUNK) - jk // CHUNK
    ok = (cdiff >= 0) & (cdiff <= B_PAST_CHUNKS)
    return jnp.where(jnp.asarray(ok)[None], tab, NEG)


def _mixer_d_kernel(q_ref, k0_ref, k1_ref, v0_ref, v1_ref, sink_ref, o_ref):
    i = pl.program_id(1)
    k_refs = (k0_ref, k1_ref)
    v_refs = (v0_ref, v1_ref)
    qpos = TQ + lax.broadcasted_iota(jnp.int32, (TQ, TQ), 0)
    bias = []
    for a in range(2):
        kpos = a * TQ + lax.broadcasted_iota(jnp.int32, (TQ, TQ), 1)
        cdiff = qpos // CHUNK - kpos // CHUNK
        ok = (cdiff >= 0) & (cdiff <= D_WINDOW_CHUNKS)
        if a == 0:
            ok = ok & (i > 0)
        bias.append(jnp.where(ok, 0.0, NEG))
    group = N_HEADS // D_KV_HEADS

    def piece(h):
        n = (h // group) * 2 + h % 2
        return slice(n * LANES, (n + 1) * LANES)

    _window_heads(q_ref, o_ref, 2,
                  lambda h, a: k_refs[a][:, piece(h)],
                  lambda h, a: v_refs[a][:, piece(h)],
                  lambda h, a: bias[a],
                  sink_of=lambda h: sink_ref[h:h + 1, 0:1])


def _mixer_d(q, k, v, sinks):
    bsz, seq, _ = q.shape
    qmap = lambda b, i: (b, i, 0)
    prev = lambda b, i: (b, jnp.maximum(i - 1, 0), 0)
    sink_tab = jnp.broadcast_to((sinks.astype(F32) * LOG2E)[:, None], (N_HEADS, LANES))
    return pl.pallas_call(
        _mixer_d_kernel,
        grid=(bsz, seq // TQ),
        in_specs=[pl.BlockSpec((None, TQ, HW), qmap),
                  pl.BlockSpec((None, TQ, HW), prev), pl.BlockSpec((None, TQ, HW), qmap),
                  pl.BlockSpec((None, TQ, HW), prev), pl.BlockSpec((None, TQ, HW), qmap),
                  pl.BlockSpec((N_HEADS, LANES), lambda b, i: (0, 0))],
        out_specs=pl.BlockSpec((None, TQ, HW), qmap),
        out_shape=jax.ShapeDtypeStruct((bsz, seq, HW), BF16),
        compiler_params=_cparams(2),
        name="mixer_d",
    )(q, k, k, v, v, sink_tab)


def _out_proj_kernel(h_ref, y0_ref, y1_ref, w0_ref, w1_ref, o_ref):
    o_ref[...] = h_ref[...] + _dot(y0_ref[...], w0_ref[...]) + _dot(y1_ref[...], w1_ref[...])


def _out_proj(h2, y0, y1, w0, w1, seq):
    m = h2.shape[0]
    tm = min(TM_PROJ, seq)
    row = lambda i: (i, 0)
    const = lambda i: (0, 0)
    return pl.pallas_call(
        _out_proj_kernel,
        grid=(m // tm,),
        in_specs=[pl.BlockSpec((tm, D_MODEL), row),
                  pl.BlockSpec((tm, HW), row), pl.BlockSpec((tm, HW), row),
                  pl.BlockSpec((HW, D_MODEL), const), pl.BlockSpec((HW, D_MODEL), const)],
        out_specs=pl.BlockSpec((tm, D_MODEL), row),
        out_shape=jax.ShapeDtypeStruct((m, D_MODEL), F32),
        compiler_params=_cparams(1),
        name="out_proj",
    )(h2, y0, y1, w0, w1)


def _ffn_kernel(h_ref, halo_ref, g_ref, wg_ref, wu_ref, cwg_ref, cwu_ref, cbg_ref, cbu_ref,
                wd_ref, go_ref, o_ref, xn_ref, xh_ref, acc_ref, hb_ref, *, nseq, final_norm):
    i = pl.program_id(0)
    f = pl.program_id(1)
    tm = h_ref.shape[0]

    @pl.when(f == 0)
    def _():
        xn_ref[...] = _rms(h_ref[...], g_ref[...]).astype(BF16)
        xh_ref[...] = _rms(halo_ref[...], g_ref[...]).astype(BF16)
        acc_ref[...] = jnp.zeros_like(acc_ref)

    halo_on = jnp.where(i % nseq == 0, 0.0, 1.0)

    def branch(w_ref, cw_ref, cb_ref):
        hb_ref[HALO:, :] = _dot(xn_ref[...], w_ref[...])
        hb_ref[:HALO, :] = _dot(xh_ref[...], w_ref[...]) * halo_on
        cw = cw_ref[...]
        return (hb_ref[pl.ds(HALO, tm), :] * cw[2:3, :]
                + hb_ref[pl.ds(HALO - 1, tm), :] * cw[1:2, :]
                + hb_ref[pl.ds(HALO - 2, tm), :] * cw[0:1, :]
                + cb_ref[...])

    yg = branch(wg_ref, cwg_ref, cbg_ref)
    yu = branch(wu_ref, cwu_ref, cbu_ref)
    act = (yg / (1.0 + jnp.exp(-yg))) * yu
    acc_ref[...] += _dot(act.astype(BF16), wd_ref[...])

    @pl.when(f == pl.num_programs(1) - 1)
    def _():
        out = h_ref[...] + acc_ref[...]
        if final_norm:
            out = _rms(out, go_ref[...])
        o_ref[...] = out


def _ffn(h2, g, w_up, conv_w, conv_b, w_down, g_out, seq, final_norm):
    m = h2.shape[0]
    tm = min(TM_FFN, seq)
    nseq = seq // tm
    nf = D_FF // TF
    row = lambda i, f: (i, 0)
    const = lambda i, f: (0, 0)
    halo = lambda i, f: (jnp.maximum(i * (tm // HALO) - 1, 0), 0)
    gcol = lambda i, f: (0, f)
    ucol = lambda i, f: (0, nf + f)
    return pl.pallas_call(
        functools.partial(_ffn_kernel, nseq=nseq, final_norm=final_norm),
        grid=(m // tm, nf),
        in_specs=[pl.BlockSpec((tm, D_MODEL), row),
                  pl.BlockSpec((HALO, D_MODEL), halo),
                  pl.BlockSpec((1, D_MODEL), const),
                  pl.BlockSpec((D_MODEL, TF), gcol), pl.BlockSpec((D_MODEL, TF), ucol),
                  pl.BlockSpec((3, TF), gcol), pl.BlockSpec((3, TF), ucol),
                  pl.BlockSpec((1, TF), gcol), pl.BlockSpec((1, TF), ucol),
                  pl.BlockSpec((TF, D_MODEL), lambda i, f: (f, 0)),
                  pl.BlockSpec((1, D_MODEL), const)],
        out_specs=pl.BlockSpec((tm, D_MODEL), row),
        out_shape=jax.ShapeDtypeStruct((m, D_MODEL), F32),
        scratch_shapes=[pltpu.VMEM((tm, D_MODEL), BF16),
                        pltpu.VMEM((HALO, D_MODEL), BF16),
                        pltpu.VMEM((tm, D_MODEL), F32),
                        pltpu.VMEM((tm + HALO, TF), F32)],
        compiler_params=_cparams(2),
        name="ffn",
    )(h2, h2, g, w_up, w_up, conv_w, conv_w, conv_b, conv_b, w_down, g_out)


def _rope_tables(seq):
    inv = 1.0 / (ROPE_THETA ** (jnp.arange(0, HEAD_DIM, 2, dtype=F32) / HEAD_DIM))
    ang = jnp.arange(seq, dtype=F32)[:, None] * inv[None, :]
    c, s = jnp.cos(ang), jnp.sin(ang)
    cos = jnp.concatenate([c, c, c, c], axis=1)
    sin = jnp.concatenate([-s, s, -s, s], axis=1)
    return cos, sin


def _pad_cols(w, width):
    return jnp.pad(w, ((0, 0), (0, width - w.shape[1])))


def _even_weights(w_in):
    o = np.cumsum([0, 512, 512, 512, IDX_HEADS * IDX_DIM, IDX_DIM, IDX_HEADS, 512, 512, 512])
    sl = [w_in[:, o[n]:o[n + 1]] for n in range(9)]
    small = jnp.concatenate([_pad_cols(sl[4], LANES), _pad_cols(sl[5], LANES)], axis=1)
    return jnp.concatenate([sl[0], sl[1], sl[2], sl[3], small, sl[6], sl[7], sl[8]],
                           axis=1).astype(BF16)


def _odd_weights(w_in):
    o = np.cumsum([0, 512, 512, 512, N_HEADS, 512, D_KV_HEADS * HEAD_DIM, D_KV_HEADS * HEAD_DIM])
    sl = [w_in[:, o[n]:o[n + 1]] for n in range(7)]
    return jnp.concatenate([sl[0], sl[1], sl[2], _pad_cols(sl[3], LANES), sl[4], sl[5], sl[6]],
                           axis=1).astype(BF16)


def _row(v, width=None):
    v = v.astype(F32)[None, :]
    return v if width is None else _pad_cols(v, width)


def kernel(x, norm_mix_g, norm_ffn_g, norm_out_g, even_w_in, even_w_out, idx_k_ln_g, idx_k_ln_b,
           rel_bias, odd_w_in, odd_w_out, forget_b, sinks, ffn_w_up, ffn_conv_w, ffn_conv_b,
           ffn_w_down):
    bsz, seq, _ = x.shape
    m = bsz * seq
    depth = norm_mix_g.shape[0]
    assert seq % TQ == 0 and seq % min(TM_PROJ, seq) == 0 and seq % min(TKB_MAX, seq) == 0
    cos, sin = _rope_tables(seq)
    h = x.reshape(m, D_MODEL)
    as3 = lambda t: t.reshape(bsz, seq, t.shape[-1])

    for layer in range(depth):
        jj = layer // 2
        g_mix = _row(norm_mix_g[layer])
        if layer % 2 == 0:
            qa, ka, va, qi, ki, wi, qb, kb, vb = _even_proj(
                h, g_mix, _even_weights(even_w_in[jj]), cos, sin,
                _row(idx_k_ln_g[jj], LANES), _row(idx_k_ln_b[jj], LANES), seq)
            kit = jnp.swapaxes(as3(ki), 1, 2)
            kie, kio = kit, jnp.roll(kit, IDX_DIM, axis=1)
            y0 = _mixer_a(as3(qi), as3(wi), kie, kio, as3(qa), as3(ka), as3(va))
            y1 = _mixer_b(as3(qb), as3(kb), as3(vb), _b_bias_table(rel_bias[jj]))
            w_out = even_w_out[jj]
        else:
            qc, kc, vc, fcum, qd, kd, vd = _odd_proj(
                h, g_mix, _odd_weights(odd_w_in[jj]), cos, sin,
                _row(forget_b[jj], LANES), seq)
            fq = as3(fcum)
            fk = jnp.swapaxes(fq[:, :, :N_HEADS], 1, 2)
            y0 = _mixer_c(as3(qc), as3(kc), as3(vc), fq, fk)
            y1 = _mixer_d(as3(qd), as3(kd), as3(vd), sinks[jj])
            w_out = odd_w_out[jj]
        w_out = w_out.astype(BF16)
        h = _out_proj(h, y0.reshape(m, HW), y1.reshape(m, HW), w_out[:HW], w_out[HW:], seq)
        h = _ffn(h, _row(norm_ffn_g[layer]), ffn_w_up[layer].astype(BF16),
                 ffn_conv_w[layer].astype(F32), _row(ffn_conv_b[layer]),
                 ffn_w_down[layer].astype(BF16), _row(norm_out_g), seq,
                 final_norm=(layer == depth - 1))
    return h.reshape(bsz, seq, D_MODEL)
```

```python
import functools

import numpy as np
import jax
import jax.numpy as jnp
from jax import lax
from jax.experimental import pallas as pl
from jax.experimental.pallas import tpu as pltpu

F32 = jnp.float32
BF16 = jnp.bfloat16

D_MODEL = 1024
HEAD_DIM = 64
CHUNK = 64
ROPE_THETA = 10000.0
EPS = 1e-6
NEG = -1e30
N_HEADS = 8
IDX_HEADS = 4
IDX_DIM = 64
IDX_TOPK_MAX = 256
B_PAST_CHUNKS = 8
B_REL_CLIP = 256
D_KV_HEADS = 2
D_WINDOW_CHUNKS = 2
D_FF = 2816
HW = N_HEADS * HEAD_DIM
SCALE = HEAD_DIM ** -0.5
LOG2E = 1.4426950408889634
QSCALE = SCALE * LOG2E

LANES = 128
VMEM_LIMIT = 56 * 1024 * 1024

TM_PROJ = 512
TM_FFN = 1024
TQ = 256
TQ_C = 256
SUB = 512
TKB_MAX = 2048
TF = 256
HALO = 16
LAZY_MARGIN = 20.0


def _cparams(n_axes):
    return pltpu.CompilerParams(
        dimension_semantics=("arbitrary",) * n_axes,
        vmem_limit_bytes=VMEM_LIMIT)


def _rms(x, g):
    ms = jnp.mean(x * x, axis=-1, keepdims=True)
    return x * lax.rsqrt(ms + EPS) * g


def _lane_iota(shape):
    return lax.broadcasted_iota(jnp.int32, shape, len(shape) - 1)


def _rope(t, cos, sin):
    w = t.shape[1]
    lane = _lane_iota(t.shape)
    fwd = pltpu.roll(t, 32, 1)
    bwd = pltpu.roll(t, w - 32, 1)
    rot = jnp.where((lane % HEAD_DIM) < (HEAD_DIM // 2), bwd, fwd)
    return t * cos + rot * sin


def _tile_lanes(t, reps):
    return t if reps == 1 else jnp.concatenate([t] * reps, axis=1)


def _head_pieces(p, sw=None):
    lane = _lane_iota(p.shape)
    lo = lane < HEAD_DIM
    q = p if sw is None else sw
    even = jnp.where(lane == HEAD_DIM, 1.0, jnp.where(lo, p, 0.0))
    odd = jnp.where(lane == 0, 1.0, jnp.where(lo, 0.0, q))
    return even, odd


def _expand_heads(t, ones_lane=False):
    pieces = []
    for j in range(t.shape[1] // LANES):
        p = t[:, j * LANES:(j + 1) * LANES]
        if ones_lane:
            pieces.extend(_head_pieces(p))
        else:
            lo = _lane_iota(p.shape) < HEAD_DIM
            pieces.append(jnp.where(lo, p, 0.0))
            pieces.append(jnp.where(lo, 0.0, p))
    return jnp.concatenate(pieces, axis=1).astype(BF16)


def _expand_kv_pair(t, ones_lane=False):
    lo = _lane_iota(t.shape) < HEAD_DIM
    sw = pltpu.roll(t, HEAD_DIM, 1)
    if ones_lane:
        e0, o0 = _head_pieces(t, sw)
        e1, o1 = _head_pieces(sw, t)
        pieces = [e0, o0, e1, o1]
    else:
        pieces = [jnp.where(lo, t, 0.0), jnp.where(lo, 0.0, sw),
                  jnp.where(lo, sw, 0.0), jnp.where(lo, 0.0, t)]
    return jnp.concatenate(pieces, axis=1).astype(BF16)


def _dot(a, b):
    return jnp.dot(a, b, preferred_element_type=F32)


def _dot_nt(a, b):
    return lax.dot_general(a, b, (((1,), (1,)), ((), ())), preferred_element_type=F32)


def _even_proj_kernel(x_ref, g_ref, w_ref, cos_ref, sin_ref, lng_ref, lnb_ref,
                      qa_ref, ka_ref, va_ref, qi_ref, ki_ref, wi_ref,
                      qb_ref, kb_ref, vb_ref):
    xn = _rms(x_ref[...], g_ref[...]).astype(BF16)
    cos = cos_ref[...]
    sin = sin_ref[...]
    cos4 = _tile_lanes(cos, 4)
    sin4 = _tile_lanes(sin, 4)

    def proj(a, b):
        return _dot(xn, w_ref[:, a:b])

    qa_ref[...] = (_rope(proj(0, 512), cos4, sin4) * QSCALE).astype(BF16)
    ka_ref[...] = _expand_heads(_rope(proj(512, 1024), cos4, sin4))
    va_ref[...] = _expand_heads(proj(1024, 1536), ones_lane=True)
    qi_ref[...] = _rope(proj(1536, 1792), _tile_lanes(cos, 2), _tile_lanes(sin, 2)).astype(BF16)
    small = proj(1792, 2048)
    kraw = small[:, :LANES]
    wi_ref[...] = small[:, LANES:]
    valid = _lane_iota(kraw.shape) < IDX_DIM
    mean = jnp.sum(kraw, axis=-1, keepdims=True) * (1.0 / IDX_DIM)
    xc = jnp.where(valid, kraw - mean, 0.0)
    var = jnp.sum(xc * xc, axis=-1, keepdims=True) * (1.0 / IDX_DIM)
    kn = xc * lax.rsqrt(var + EPS) * lng_ref[...] + lnb_ref[...]
    ki_ref[...] = _rope(kn, cos, sin).astype(BF16)
    qb_ref[...] = (proj(2048, 2560) * QSCALE).astype(BF16)
    kb_ref[...] = _expand_heads(proj(2560, 3072))
    vb_ref[...] = _expand_heads(proj(3072, 3584), ones_lane=True)


def _even_proj(x2, g, w, cos, sin, lng, lnb, seq):
    m = x2.shape[0]
    tm = min(TM_PROJ, seq)
    nseq = seq // tm
    row = lambda i: (i, 0)
    const = lambda i: (0, 0)
    tab = lambda i: (i % nseq, 0)
    widths = [(512, BF16), (1024, BF16), (1024, BF16), (256, BF16), (128, BF16),
              (128, F32), (512, BF16), (1024, BF16), (1024, BF16)]
    return pl.pallas_call(
        _even_proj_kernel,
        grid=(m // tm,),
        in_specs=[pl.BlockSpec((tm, D_MODEL), row),
                  pl.BlockSpec((1, D_MODEL), const),
                  pl.BlockSpec(w.shape, const),
                  pl.BlockSpec((tm, LANES), tab),
                  pl.BlockSpec((tm, LANES), tab),
                  pl.BlockSpec((1, LANES), const),
                  pl.BlockSpec((1, LANES), const)],
        out_specs=[pl.BlockSpec((tm, wd), row) for wd, _ in widths],
        out_shape=[jax.ShapeDtypeStruct((m, wd), dt) for wd, dt in widths],
        compiler_params=_cparams(1),
        name="even_proj",
    )(x2, g, w, cos, sin, lng, lnb)


def _odd_proj_kernel(x_ref, g_ref, w_ref, cos_ref, sin_ref, fb_ref, tril_ref,
                     qc_ref, kc_ref, vc_ref, f_ref, qd_ref, kd_ref, vd_ref,
                     carry_ref, *, nseq):
    i = pl.program_id(0)
    xn = _rms(x_ref[...], g_ref[...]).astype(BF16)
    cos = cos_ref[...]
    sin = sin_ref[...]

    def proj(a, b):
        return _dot(xn, w_ref[:, a:b])

    qc_ref[...] = (proj(0, 512) * QSCALE).astype(BF16)
    kc_ref[...] = _expand_heads(proj(512, 1024))
    vc_ref[...] = _expand_heads(proj(1024, 1536), ones_lane=True)

    z = proj(1536, 1664) + fb_ref[...]
    lf = jnp.minimum(z, 0.0) - jnp.log(1.0 + jnp.exp(-jnp.abs(z)))
    lf = jnp.where(_lane_iota(lf.shape) < N_HEADS, lf, 0.0)
    hi = lf.astype(BF16)
    r1 = lf - hi.astype(F32)
    mid = r1.astype(BF16)
    lo = (r1 - mid.astype(F32)).astype(BF16)
    tril = tril_ref[...]
    csum = _dot(tril, hi) + _dot(tril, mid) + _dot(tril, lo)

    @pl.when(i % nseq == 0)
    def _():
        carry_ref[...] = jnp.zeros_like(carry_ref)

    fcum = csum + carry_ref[0:1, :]
    f_ref[...] = fcum * LOG2E
    carry_ref[...] = jnp.broadcast_to(fcum[fcum.shape[0] - 1:, :], carry_ref.shape)

    qd_ref[...] = (_rope(proj(1664, 2176), _tile_lanes(cos, 4), _tile_lanes(sin, 4))
                   * QSCALE).astype(BF16)
    kv = proj(2176, 2432)
    kd_ref[...] = _expand_kv_pair(_rope(kv[:, :LANES], cos, sin))
    vd_ref[...] = _expand_kv_pair(kv[:, LANES:], ones_lane=True)


def _odd_proj(x2, g, w, cos, sin, fb, seq):
    m = x2.shape[0]
    tm = min(TM_PROJ, seq)
    nseq = seq // tm
    row = lambda i: (i, 0)
    const = lambda i: (0, 0)
    tab = lambda i: (i % nseq, 0)
    tril = jnp.tril(jnp.ones((tm, tm), F32)).astype(BF16)
    widths = [(512, BF16), (1024, BF16), (1024, BF16), (128, F32),
              (512, BF16), (512, BF16), (512, BF16)]
    return pl.pallas_call(
        functools.partial(_odd_proj_kernel, nseq=nseq),
        grid=(m // tm,),
        in_specs=[pl.BlockSpec((tm, D_MODEL), row),
                  pl.BlockSpec((1, D_MODEL), const),
                  pl.BlockSpec(w.shape, const),
                  pl.BlockSpec((tm, LANES), tab),
                  pl.BlockSpec((tm, LANES), tab),
                  pl.BlockSpec((1, LANES), const),
                  pl.BlockSpec((tm, tm), const)],
        out_specs=[pl.BlockSpec((tm, wd), row) for wd, _ in widths],
        out_shape=[jax.ShapeDtypeStruct((m, wd), dt) for wd, dt in widths],
        scratch_shapes=[pltpu.VMEM((8, LANES), F32)],
        compiler_params=_cparams(1),
        name="odd_proj",
    )(x2, g, w, cos, sin, fb, tril)


def _denominator_lane(h):
    return HEAD_DIM if h % 2 == 0 else 0


def _flash_heads(q_ref, k_ref, v_ref, rows, acc_ref, m_ref, logit_fn):
    nrep = SUB // LANES
    s_next = _head_scores(q_ref, k_ref, rows, 0)
    for h in range(N_HEADS):
        s_raw = s_next
        if h + 1 < N_HEADS:
            s_next = _head_scores(q_ref, k_ref, rows, h + 1)
        s, shift = logit_fn(h, s_raw)
        m_prev = m_ref[h]
        m_cur = jnp.max(s, axis=1, keepdims=True)
        if shift is not None:
            m_cur = m_cur + shift
        m_new = jnp.maximum(m_prev, m_cur)
        alpha = jnp.exp2(m_prev - m_new)
        off = m_new if shift is None else m_new - shift
        p = jnp.exp2(s - _tile_lanes(off, nrep))
        m_ref[h] = m_new
        cols = slice(h * LANES, (h + 1) * LANES)
        acc_ref[:, cols] = acc_ref[:, cols] * alpha + _dot(p.astype(BF16), v_ref[rows, cols])


def _head_scores(q_ref, k_ref, rows, h):
    jp = h // 2
    return _dot_nt(q_ref[:, jp * LANES:(jp + 1) * LANES], k_ref[rows, h * LANES:(h + 1) * LANES])


def _flash_heads_lazy(q_ref, k_ref, v_ref, rows, acc_ref, m_ref, tmp_ref, logit_fn):
    nrep = SUB // LANES
    worst = None
    pending = [_head_scores(q_ref, k_ref, rows, h) for h in range(2)]
    for h in range(N_HEADS):
        if h + 2 < N_HEADS:
            pending.append(_head_scores(q_ref, k_ref, rows, h + 2))
        s, shift = logit_fn(h, pending.pop(0))
        ref = m_ref[h] if shift is None else m_ref[h] - shift
        lane_max = functools.reduce(
            jnp.maximum, [s[:, t * LANES:(t + 1) * LANES] for t in range(nrep)])
        over = lane_max - ref
        worst = over if worst is None else jnp.maximum(worst, over)
        p = jnp.exp2(s - _tile_lanes(ref, nrep))
        cols = slice(h * LANES, (h + 1) * LANES)
        tmp_ref[:, cols] = _dot(p.astype(BF16), v_ref[rows, cols])
    exceeded = jnp.max(worst) > LAZY_MARGIN

    @pl.when(jnp.logical_not(exceeded))
    def _():
        acc_ref[...] += tmp_ref[...]

    @pl.when(exceeded)
    def _():
        _flash_heads(q_ref, k_ref, v_ref, rows, acc_ref, m_ref, logit_fn)


def _flash_init(acc_ref, m_ref):
    acc_ref[...] = jnp.zeros_like(acc_ref)
    m_ref[...] = jnp.full(m_ref.shape, NEG, F32)


def _pair_output(even, odd, l_even, l_odd):
    lo = _lane_iota(even.shape) < HEAD_DIM
    return jnp.where(lo, even / l_even, odd / l_odd)


def _flash_finish(o_ref, acc_ref):
    for jp in range(N_HEADS // 2):
        even = acc_ref[:, (2 * jp) * LANES:(2 * jp + 1) * LANES]
        odd = acc_ref[:, (2 * jp + 1) * LANES:(2 * jp + 2) * LANES]
        le = even[:, _denominator_lane(0):_denominator_lane(0) + 1]
        lo_ = odd[:, _denominator_lane(1):_denominator_lane(1) + 1]
        o_ref[:, jp * LANES:(jp + 1) * LANES] = _pair_output(even, odd, le, lo_).astype(o_ref.dtype)


def _kv_tiles(seq):
    tkb = min(TKB_MAX, seq)
    return tkb, seq // tkb, tkb // SUB


def _mixer_c_kernel(q_ref, k_ref, v_ref, fq_ref, fk_ref, o_ref, acc_ref, m_ref, fqr_ref,
                    *, tq, tkb, nsub):
    i = pl.program_id(1)
    j = pl.program_id(2)
    q_lo = i * tq
    last_sub = (q_lo + tq - 1) // SUB
    first_masked = q_lo // SUB
    jmax = last_sub // nsub

    @pl.when(j == 0)
    def _():
        _flash_init(acc_ref, m_ref)
        for h in range(N_HEADS):
            fqr_ref[h] = jnp.broadcast_to(fq_ref[:, h:h + 1], (tq, LANES))

    def step(c, masked):
        rows = pl.ds(pl.multiple_of(c * SUB, SUB), SUB)

        def logit_fn(h, s):
            s = s - fk_ref[h:h + 1, rows]
            if masked:
                kpos = j * tkb + c * SUB + lax.broadcasted_iota(jnp.int32, s.shape, 1)
                qpos = q_lo + lax.broadcasted_iota(jnp.int32, s.shape, 0)
                s = jnp.where(kpos <= qpos, s, NEG)
            return s, fqr_ref[h]

        _flash_heads(q_ref, k_ref, v_ref, rows, acc_ref, m_ref, logit_fn)

    @pl.when(j <= jmax)
    def _():
        lo = j * nsub
        n_full = jnp.clip(first_masked - lo, 0, nsub)
        n_all = jnp.clip(last_sub + 1 - lo, 0, nsub)

        def body(c, carry):
            step(c, False)
            return carry

        def masked_body(c, carry):
            step(c, True)
            return carry

        lax.fori_loop(0, n_full, body, 0)
        lax.fori_loop(n_full, n_all, masked_body, 0)

        @pl.when(j == jmax)
        def _():
            _flash_finish(o_ref, acc_ref)


def _mixer_c(q, k, v, fq, fk):
    bsz, seq, _ = q.shape
    tq = min(TQ_C, seq)
    tkb, nkb, nsub = _kv_tiles(seq)
    kv_idx = lambda b, i, j: (b, jnp.minimum(j, (i * tq + tq - 1) // tkb), 0)
    return pl.pallas_call(
        functools.partial(_mixer_c_kernel, tq=tq, tkb=tkb, nsub=nsub),
        grid=(bsz, seq // tq, nkb),
        in_specs=[pl.BlockSpec((None, tq, HW), lambda b, i, j: (b, i, 0)),
                  pl.BlockSpec((None, tkb, 2 * HW), kv_idx),
                  pl.BlockSpec((None, tkb, 2 * HW), kv_idx),
                  pl.BlockSpec((None, tq, LANES), lambda b, i, j: (b, i, 0)),
                  pl.BlockSpec((None, N_HEADS, tkb),
                               lambda b, i, j: (b, 0, jnp.minimum(j, (i * tq + tq - 1) // tkb)))],
        out_specs=pl.BlockSpec((None, tq, HW), lambda b, i, j: (b, i, 0)),
        out_shape=jax.ShapeDtypeStruct((bsz, seq, HW), BF16),
        scratch_shapes=[pltpu.VMEM((tq, 2 * HW), F32),
                        pltpu.VMEM((N_HEADS, tq, LANES), F32),
                        pltpu.VMEM((N_HEADS, tq, LANES), F32)],
        compiler_params=_cparams(3),
        name="mixer_c",
    )(q, k, v, fq, fk)


MIN_NORMAL_KEY = 0x00800000


def _key_flip(bits):
    return jnp.where(bits >= 0, bits, bits ^ jnp.int32(0x7FFFFFFF))


def _float_to_key(f):
    return _key_flip(pltpu.bitcast(f, jnp.int32))


def _key_to_float(k):
    return pltpu.bitcast(_key_flip(k), F32)


def _mixer_a_kernel(qi_ref, wi_ref, kie_ref, kio_ref, q_ref, k_ref, v_ref, tri_ref,
                    o_ref, s_ref, gmax_ref, cnt_ref, acc_ref, m_ref, tmp_ref,
                    *, tkb, nsub, topk):
    i = pl.program_id(1)
    j = pl.program_id(2)
    q_lo = i * TQ
    last_sub = q_lo // SUB
    jmax = last_sub // nsub
    n_sub = last_sub + 1
    nrep = SUB // LANES
    half = TQ // 2

    def admissible(c_glob, shape):
        kpos = c_glob * SUB + lax.broadcasted_iota(jnp.int32, shape, 1)
        qpos = q_lo + lax.broadcasted_iota(jnp.int32, shape, 0)
        return (kpos // CHUNK) <= (qpos // CHUNK)

    def s_load(c):
        return jnp.concatenate([s_ref[c * nrep + t] for t in range(nrep)], axis=1)

    def s_store(c, val):
        for t in range(nrep):
            s_ref[c * nrep + t] = val[:, t * LANES:(t + 1) * LANES]

    def masked_logits(c_glob):
        return lambda h, s: (s + s_load(c_glob), None)

    @pl.when(j == 0)
    def _():
        _flash_init(acc_ref, m_ref)
        cnt_ref[...] = jnp.zeros_like(cnt_ref)
        gmax_ref[...] = jnp.full(gmax_ref.shape, -jnp.inf, F32)

        def score(c, masked):
            cols = pl.ds(pl.multiple_of(c * SUB, SUB), SUB)
            tot = None
            for pair in range(IDX_HEADS // 2):
                qp = qi_ref[:, pair * LANES:(pair + 1) * LANES]
                for e in range(2):
                    h = 2 * pair + e
                    kt = (kie_ref if e == 0 else kio_ref)[:, cols]
                    term = jnp.maximum(_dot(qp, kt), 0.0) * wi_ref[:, h:h + 1]
                    tot = term if tot is None else tot + term
            if masked:
                tot = jnp.where(admissible(c, tot.shape), tot, NEG)
            s_store(c, tot)
            for par in range(2):
                g = gmax_ref[par]
                for t in range(par, nrep, 2):
                    g = jnp.maximum(g, tot[:, t * LANES:(t + 1) * LANES])
                gmax_ref[par] = g

        def score_body(c, carry):
            score(c, False)
            return carry

        lax.fori_loop(0, last_sub, score_body, 0)
        score(last_sub, True)

        def count(cand, strict):
            outs = []
            for hh in range(2):
                r0 = hh * half
                ch = cand[r0:r0 + half]

                def body(c, acc):
                    for t in range(nrep):
                        blk = s_ref[c * nrep + t, r0:r0 + half, :]
                        hit = (blk > ch) if strict else (blk >= ch)
                        acc = acc + jnp.where(hit, 1.0, 0.0)
                    return acc

                acc = lax.fori_loop(0, n_sub, body, jnp.zeros((half, LANES), F32))
                outs.append(jnp.broadcast_to(jnp.sum(acc, axis=1, keepdims=True), (half, LANES)))
            return jnp.concatenate(outs, axis=0)

        ga = gmax_ref[0]
        gb = gmax_ref[1]
        lo_f = jnp.min(jnp.minimum(ga, gb), axis=1, keepdims=True)
        hi_f = jnp.max(jnp.maximum(ga, gb), axis=1, keepdims=True)
        lo0 = jnp.broadcast_to(_float_to_key(lo_f), (TQ, LANES))
        hi0 = jnp.broadcast_to(_float_to_key(hi_f), (TQ, LANES))

        def unresolved(st):
            lo, hi = st
            return jnp.max(jnp.where(lo < hi, 1.0, 0.0)) > 0.5

        def narrow(lo, hi, mid):
            n_ge = count(_key_to_float(mid), False)
            ge = n_ge >= topk
            below = jnp.where(mid == MIN_NORMAL_KEY, 0, mid - 1)
            return (jnp.where(ge, mid, lo),
                    jnp.where(n_ge == topk, mid, jnp.where(ge, hi, below)))

        def bisect(st):
            lo, hi = st
            by_key = (lo >> 1) + (hi >> 1) + ((lo | hi) & 1)
            lo_f = _key_to_float(lo)
            hi_f = _key_to_float(hi)
            by_val = _float_to_key(lo_f + (hi_f - lo_f) * 0.5)
            by_val = jnp.minimum(jnp.maximum(by_val, lo + 1), hi)
            same_binade = (pltpu.bitcast(lo_f, jnp.int32) >> 23) == (pltpu.bitcast(hi_f, jnp.int32) >> 23)
            mid = jnp.where(same_binade, by_key, by_val)
            mid = jnp.where((lo < 0) & (hi >= 0), 0, mid)
            mid = jnp.where((lo == 0) & (hi >= MIN_NORMAL_KEY), MIN_NORMAL_KEY, mid)
            mid = jnp.where(lo < hi, mid, lo)
            return narrow(lo, hi, mid)

        lo, _ = lax.while_loop(unresolved, bisect, narrow(lo0, hi0, hi0))
        tau1 = _key_to_float(lo)
        keep1 = topk - count(tau1, True)
        tau = _tile_lanes(tau1, nrep)
        keep = _tile_lanes(keep1, nrep)

        def to_mask(c, masked):
            blk = s_load(c)
            eq = blk == tau
            rank = (_tile_lanes(cnt_ref[...], nrep)
                    + _dot(jnp.where(eq, 1.0, 0.0).astype(BF16), tri_ref[...]))
            keep_tie = jnp.where(rank <= keep, 0.0, NEG)
            bias = jnp.where(blk > tau, 0.0, jnp.where(eq, keep_tie, NEG))
            if masked:
                bias = jnp.where(admissible(c, bias.shape), bias, NEG)
            cnt_ref[...] = jnp.broadcast_to(rank[:, SUB - 1:SUB], (TQ, LANES))
            s_store(c, bias)

        def mask_body(c, carry):
            to_mask(c, False)
            return carry

        lax.fori_loop(0, last_sub, mask_body, 0)
        to_mask(last_sub, True)

        _flash_heads(q_ref, k_ref, v_ref, pl.ds(0, SUB), acc_ref, m_ref, masked_logits(0))

    @pl.when(j <= jmax)
    def _():
        def body(c, carry):
            rows = pl.ds(pl.multiple_of(c * SUB, SUB), SUB)
            _flash_heads_lazy(q_ref, k_ref, v_ref, rows, acc_ref, m_ref, tmp_ref,
                              masked_logits(j * nsub + c))
            return carry

        lax.fori_loop(jnp.where(j == 0, 1, 0), jnp.minimum(n_sub - j * nsub, nsub), body, 0)

        @pl.when(j == jmax)
        def _():
            _flash_finish(o_ref, acc_ref)


def _mixer_a(qi, wi, kie, kio, q, k, v):
    bsz, seq, _ = q.shape
    tkb, nkb, nsub = _kv_tiles(seq)
    topk = min(IDX_TOPK_MAX, seq // 4)
    tri = jnp.triu(jnp.ones((SUB, SUB), F32)).astype(BF16)
    qmap = lambda b, i, j: (b, i, 0)
    kv_idx = lambda b, i, j: (b, jnp.minimum(j, (i * TQ) // tkb), 0)
    res = lambda b, i, j: (b, 0, 0)
    return pl.pallas_call(
        functools.partial(_mixer_a_kernel, tkb=tkb, nsub=nsub, topk=topk),
        grid=(bsz, seq // TQ, nkb),
        in_specs=[pl.BlockSpec((None, TQ, IDX_HEADS * IDX_DIM), qmap),
                  pl.BlockSpec((None, TQ, LANES), qmap),
                  pl.BlockSpec((None, LANES, seq), res),
                  pl.BlockSpec((None, LANES, seq), res),
                  pl.BlockSpec((None, TQ, HW), qmap),
                  pl.BlockSpec((None, tkb, 2 * HW), kv_idx),
                  pl.BlockSpec((None, tkb, 2 * HW), kv_idx),
                  pl.BlockSpec((SUB, SUB), lambda b, i, j: (0, 0))],
        out_specs=pl.BlockSpec((None, TQ, HW), qmap),
        out_shape=jax.ShapeDtypeStruct((bsz, seq, HW), BF16),
        scratch_shapes=[pltpu.VMEM((seq // LANES, TQ, LANES), F32),
                        pltpu.VMEM((2, TQ, LANES), F32),
                        pltpu.VMEM((TQ, LANES), F32),
                        pltpu.VMEM((TQ, 2 * HW), F32),
                        pltpu.VMEM((N_HEADS, TQ, LANES), F32),
                        pltpu.VMEM((TQ, 2 * HW), F32)],
        compiler_params=_cparams(3),
        name="mixer_a",
    )(qi, wi, kie, kio, q, k, v, tri)


B_WIN = 3


def _window_heads(q_ref, o_ref, n_win, k_of, v_of, bias_of, sink_of=None):
    def scores(h):
        jp = h // 2
        qp = q_ref[:, jp * LANES:(jp + 1) * LANES]
        return [_dot_nt(qp, k_of(h, a)) + bias_of(h, a) for a in range(n_win)]

    s_next = scores(0)
    done = []
    for h in range(N_HEADS):
        s = s_next
        if h + 1 < N_HEADS:
            s_next = scores(h + 1)
        m = functools.reduce(jnp.maximum, [jnp.max(t, axis=1, keepdims=True) for t in s])
        if sink_of is not None:
            m = jnp.maximum(m, sink_of(h))
        pv = functools.reduce(
            jnp.add, [_dot(jnp.exp2(s[a] - m).astype(BF16), v_of(h, a)) for a in range(n_win)])
        lane = _denominator_lane(h)
        l = pv[:, lane:lane + 1]
        if sink_of is not None:
            l = l + jnp.exp2(sink_of(h) - m)
        done.append((pv, l))
        if h % 2 == 1:
            (pe, le), (po, lo_) = done[-2], done[-1]
            jp = h // 2
            o_ref[:, jp * LANES:(jp + 1) * LANES] = _pair_output(pe, po, le, lo_).astype(o_ref.dtype)


def _mixer_b_kernel(q_ref, k0_ref, k1_ref, k2_ref, v0_ref, v1_ref, v2_ref, bias_ref, o_ref):
    i = pl.program_id(1)
    k_refs = (k0_ref, k1_ref, k2_ref)
    v_refs = (v0_ref, v1_ref, v2_ref)
    pad = [jnp.where(i - (B_WIN - 1 - a) >= 0, 0.0, NEG) for a in range(B_WIN)]
    hc = lambda h: slice(h * LANES, (h + 1) * LANES)
    _window_heads(q_ref, o_ref, B_WIN,
                  lambda h, a: k_refs[a][:, hc(h)],
                  lambda h, a: v_refs[a][:, hc(h)],
                  lambda h, a: bias_ref[h, :, a * TQ:(a + 1) * TQ] + pad[a])


def _mixer_b(q, k, v, bias_tab):
    bsz, seq, _ = q.shape
    qmap = lambda b, i: (b, i, 0)
    kmaps = [functools.partial(lambda b, i, d: (b, jnp.maximum(i - d, 0), 0), d=B_WIN - 1 - a)
             for a in range(B_WIN)]
    kspec = [pl.BlockSpec((None, TQ, 2 * HW), km) for km in kmaps]
    return pl.pallas_call(
        _mixer_b_kernel,
        grid=(bsz, seq // TQ),
        in_specs=[pl.BlockSpec((None, TQ, HW), qmap)] + kspec + kspec
                 + [pl.BlockSpec(bias_tab.shape, lambda b, i: (0, 0, 0))],
        out_specs=pl.BlockSpec((None, TQ, HW), qmap),
        out_shape=jax.ShapeDtypeStruct((bsz, seq, HW), BF16),
        compiler_params=_cparams(2),
        name="mixer_b",
    )(q, k, k, k, v, v, v, bias_tab)


def _b_bias_table(rel_bias):
    win = B_WIN * TQ
    span = TQ + win
    dist = (win - 1) - np.arange(span)
    idx = np.clip(dist, -B_REL_CLIP, B_REL_CLIP) + B_REL_CLIP
    v = rel_bias.astype(F32)[:, idx] * LOG2E
    skew = jnp.tile(v, (1, TQ + 1))[:, :TQ * (span + 1)].reshape(N_HEADS, TQ, span + 1)
    tab = skew[:, ::-1, :win]
    t = np.arange(TQ)[:, None]
    jk = np.arange(win)[None, :]
    cdiff = (t // CHUNK + (B_WIN - 1) * TQ // CHUNK) - jk // CHUNK
    ok = (cdiff >= 0) & (cdiff <= B_PAST_CHUNKS)
    return jnp.where(jnp.asarray(ok)[None], tab, NEG)


def _mixer_d_kernel(q_ref, k0_ref, k1_ref, v0_ref, v1_ref, sink_ref, o_ref):
    i = pl.program_id(1)
    k_refs = (k0_ref, k1_ref)
    v_refs = (v0_ref, v1_ref)
    qpos = TQ + lax.broadcasted_iota(jnp.int32, (TQ, TQ), 0)
    bias = []
    for a in range(2):
        kpos = a * TQ + lax.broadcasted_iota(jnp.int32, (TQ, TQ), 1)
        cdiff = qpos // CHUNK - kpos // CHUNK
        ok = (cdiff >= 0) & (cdiff <= D_WINDOW_CHUNKS)
        if a == 0:
            ok = ok & (i > 0)
        bias.append(jnp.where(ok, 0.0, NEG))
    group = N_HEADS // D_KV_HEADS

    def piece(h):
        n = (h // group) * 2 + h % 2
        return slice(n * LANES, (n + 1) * LANES)

    _window_heads(q_ref, o_ref, 2,
                  lambda h, a: k_refs[a][:, piece(h)],
                  lambda h, a: v_refs[a][:, piece(h)],
                  lambda h, a: bias[a],
                  sink_of=lambda h: sink_ref[h:h + 1, 0:1])


def _mixer_d(q, k, v, sinks):
    bsz, seq, _ = q.shape
    qmap = lambda b, i: (b, i, 0)
    prev = lambda b, i: (b, jnp.maximum(i - 1, 0), 0)
    sink_tab = jnp.broadcast_to((sinks.astype(F32) * LOG2E)[:, None], (N_HEADS, LANES))
    return pl.pallas_call(
        _mixer_d_kernel,
        grid=(bsz, seq // TQ),
        in_specs=[pl.BlockSpec((None, TQ, HW), qmap),
                  pl.BlockSpec((None, TQ, HW), prev), pl.BlockSpec((None, TQ, HW), qmap),
                  pl.BlockSpec((None, TQ, HW), prev), pl.BlockSpec((None, TQ, HW), qmap),
                  pl.BlockSpec((N_HEADS, LANES), lambda b, i: (0, 0))],
        out_specs=pl.BlockSpec((None, TQ, HW), qmap),
        out_shape=jax.ShapeDtypeStruct((bsz, seq, HW), BF16),
        compiler_params=_cparams(2),
        name="mixer_d",
    )(q, k, k, v, v, sink_tab)


def _out_proj_kernel(h_ref, y0_ref, y1_ref, w0_ref, w1_ref, o_ref):
    o_ref[...] = h_ref[...] + _dot(y0_ref[...], w0_ref[...]) + _dot(y1_ref[...], w1_ref[...])


def _out_proj(h2, y0, y1, w0, w1, seq):
    m = h2.shape[0]
    tm = min(TM_PROJ, seq)
    row = lambda i: (i, 0)
    const = lambda i: (0, 0)
    return pl.pallas_call(
        _out_proj_kernel,
        grid=(m // tm,),
        in_specs=[pl.BlockSpec((tm, D_MODEL), row),
                  pl.BlockSpec((tm, HW), row), pl.BlockSpec((tm, HW), row),
                  pl.BlockSpec((HW, D_MODEL), const), pl.BlockSpec((HW, D_MODEL), const)],
        out_specs=pl.BlockSpec((tm, D_MODEL), row),
        out_shape=jax.ShapeDtypeStruct((m, D_MODEL), F32),
        compiler_params=_cparams(1),
        name="out_proj",
    )(h2, y0, y1, w0, w1)


def _ffn_kernel(h_ref, halo_ref, g_ref, wg_ref, wu_ref, cwg_ref, cwu_ref, cbg_ref, cbu_ref,
                wd_ref, go_ref, o_ref, xn_ref, xh_ref, acc_ref, hb_ref, *, nseq, final_norm):
    i = pl.program_id(0)
    f = pl.program_id(1)
    tm = h_ref.shape[0]

    @pl.when(f == 0)
    def _():
        xn_ref[...] = _rms(h_ref[...], g_ref[...]).astype(BF16)
        xh_ref[...] = _rms(halo_ref[...], g_ref[...]).astype(BF16)
        acc_ref[...] = jnp.zeros_like(acc_ref)

    halo_on = jnp.where(i % nseq == 0, 0.0, 1.0)

    def branch(w_ref, cw_ref, cb_ref):
        hb_ref[HALO:, :] = _dot(xn_ref[...], w_ref[...])
        hb_ref[:HALO, :] = _dot(xh_ref[...], w_ref[...]) * halo_on
        cw = cw_ref[...]
        return (hb_ref[pl.ds(HALO, tm), :] * cw[2:3, :]
                + hb_ref[pl.ds(HALO - 1, tm), :] * cw[1:2, :]
                + hb_ref[pl.ds(HALO - 2, tm), :] * cw[0:1, :]
                + cb_ref[...])

    yg = branch(wg_ref, cwg_ref, cbg_ref)
    yu = branch(wu_ref, cwu_ref, cbu_ref)
    act = (yg / (1.0 + jnp.exp(-yg))) * yu
    acc_ref[...] += _dot(act.astype(BF16), wd_ref[...])

    @pl.when(f == pl.num_programs(1) - 1)
    def _():
        out = h_ref[...] + acc_ref[...]
        if final_norm:
            out = _rms(out, go_ref[...])
        o_ref[...] = out


def _ffn(h2, g, w_up, conv_w, conv_b, w_down, g_out, seq, final_norm):
    m = h2.shape[0]
    tm = min(TM_FFN, seq)
    nseq = seq // tm
    nf = D_FF // TF
    row = lambda i, f: (i, 0)
    const = lambda i, f: (0, 0)
    halo = lambda i, f: (jnp.maximum(i * (tm // HALO) - 1, 0), 0)
    gcol = lambda i, f: (0, f)
    ucol = lambda i, f: (0, nf + f)
    return pl.pallas_call(
        functools.partial(_ffn_kernel, nseq=nseq, final_norm=final_norm),
        grid=(m // tm, nf),
        in_specs=[pl.BlockSpec((tm, D_MODEL), row),
                  pl.BlockSpec((HALO, D_MODEL), halo),
                  pl.BlockSpec((1, D_MODEL), const),
                  pl.BlockSpec((D_MODEL, TF), gcol), pl.BlockSpec((D_MODEL, TF), ucol),
                  pl.BlockSpec((3, TF), gcol), pl.BlockSpec((3, TF), ucol),
                  pl.BlockSpec((1, TF), gcol), pl.BlockSpec((1, TF), ucol),
                  pl.BlockSpec((TF, D_MODEL), lambda i, f: (f, 0)),
                  pl.BlockSpec((1, D_MODEL), const)],
        out_specs=pl.BlockSpec((tm, D_MODEL), row),
        out_shape=jax.ShapeDtypeStruct((m, D_MODEL), F32),
        scratch_shapes=[pltpu.VMEM((tm, D_MODEL), BF16),
                        pltpu.VMEM((HALO, D_MODEL), BF16),
                        pltpu.VMEM((tm, D_MODEL), F32),
                        pltpu.VMEM((tm + HALO, TF), F32)],
        compiler_params=_cparams(2),
        name="ffn",
    )(h2, h2, g, w_up, w_up, conv_w, conv_w, conv_b, conv_b, w_down, g_out)


def _rope_tables(seq):
    inv = 1.0 / (ROPE_THETA ** (jnp.arange(0, HEAD_DIM, 2, dtype=F32) / HEAD_DIM))
    ang = jnp.arange(seq, dtype=F32)[:, None] * inv[None, :]
    c, s = jnp.cos(ang), jnp.sin(ang)
    cos = jnp.concatenate([c, c, c, c], axis=1)
    sin = jnp.concatenate([-s, s, -s, s], axis=1)
    return cos, sin


def _pad_cols(w, width):
    return jnp.pad(w, ((0, 0), (0, width - w.shape[1])))


def _even_weights(w_in):
    o = np.cumsum([0, 512, 512, 512, IDX_HEADS * IDX_DIM, IDX_DIM, IDX_HEADS, 512, 512, 512])
    sl = [w_in[:, o[n]:o[n + 1]] for n in range(9)]
    small = jnp.concatenate([_pad_cols(sl[4], LANES), _pad_cols(sl[5], LANES)], axis=1)
    return jnp.concatenate([sl[0], sl[1], sl[2], sl[3], small, sl[6], sl[7], sl[8]],
                           axis=1).astype(BF16)


def _odd_weights(w_in):
    o = np.cumsum([0, 512, 512, 512, N_HEADS, 512, D_KV_HEADS * HEAD_DIM, D_KV_HEADS * HEAD_DIM])
    sl = [w_in[:, o[n]:o[n + 1]] for n in range(7)]
    return jnp.concatenate([sl[0], sl[1], sl[2], _pad_cols(sl[3], LANES), sl[4], sl[5], sl[6]],
                           axis=1).astype(BF16)


def _row(v, width=None):
    v = v.astype(F32)[None, :]
    return v if width is None else _pad_cols(v, width)


def kernel(x, norm_mix_g, norm_ffn_g, norm_out_g, even_w_in, even_w_out, idx_k_ln_g, idx_k_ln_b,
           rel_bias, odd_w_in, odd_w_out, forget_b, sinks, ffn_w_up, ffn_conv_w, ffn_conv_b,
           ffn_w_down):
    bsz, seq, _ = x.shape
    m = bsz * seq
    depth = norm_mix_g.shape[0]
    assert seq % TQ == 0 and seq % min(TM_PROJ, seq) == 0 and seq % min(TKB_MAX, seq) == 0
    cos, sin = _rope_tables(seq)
    h = x.reshape(m, D_MODEL)
    as3 = lambda t: t.reshape(bsz, seq, t.shape[-1])

    for layer in range(depth):
        jj = layer // 2
        g_mix = _row(norm_mix_g[layer])
        if layer % 2 == 0:
            qa, ka, va, qi, ki, wi, qb, kb, vb = _even_proj(
                h, g_mix, _even_weights(even_w_in[jj]), cos, sin,
                _row(idx_k_ln_g[jj], LANES), _row(idx_k_ln_b[jj], LANES), seq)
            kit = jnp.swapaxes(as3(ki), 1, 2)
            kie, kio = kit, jnp.roll(kit, IDX_DIM, axis=1)
            y0 = _mixer_a(as3(qi), as3(wi), kie, kio, as3(qa), as3(ka), as3(va))
            y1 = _mixer_b(as3(qb), as3(kb), as3(vb), _b_bias_table(rel_bias[jj]))
            w_out = even_w_out[jj]
        else:
            qc, kc, vc, fcum, qd, kd, vd = _odd_proj(
                h, g_mix, _odd_weights(odd_w_in[jj]), cos, sin,
                _row(forget_b[jj], LANES), seq)
            fq = as3(fcum)
            fk = jnp.swapaxes(fq[:, :, :N_HEADS], 1, 2)
            y0 = _mixer_c(as3(qc), as3(kc), as3(vc), fq, fk)
            y1 = _mixer_d(as3(qd), as3(kd), as3(vd), sinks[jj])
            w_out = odd_w_out[jj]
        w_out = w_out.astype(BF16)
        h = _out_proj(h, y0.reshape(m, HW), y1.reshape(m, HW), w_out[:HW], w_out[HW:], seq)
        h = _ffn(h, _row(norm_ffn_g[layer]), ffn_w_up[layer].astype(BF16),
                 ffn_conv_w[layer].astype(F32), _row(ffn_conv_b[layer]),
                 ffn_w_down[layer].astype(BF16), _row(norm_out_g), seq,
                 final_norm=(layer == depth - 1))
    return h.reshape(bsz, seq, D_MODEL)
```

```python
import functools

import numpy as np
import jax
import jax.numpy as jnp
from jax import lax
from jax.experimental import pallas as pl
from jax.experimental.pallas import tpu as pltpu

F32 = jnp.float32
BF16 = jnp.bfloat16

D_MODEL = 1024
HEAD_DIM = 64
CHUNK = 64
ROPE_THETA = 10000.0
EPS = 1e-6
NEG = -1e30
N_HEADS = 8
IDX_HEADS = 4
IDX_DIM = 64
IDX_TOPK_MAX = 256
B_PAST_CHUNKS = 8
B_REL_CLIP = 256
D_KV_HEADS = 2
D_WINDOW_CHUNKS = 2
D_FF = 2816
HW = N_HEADS * HEAD_DIM
SCALE = HEAD_DIM ** -0.5
LOG2E = 1.4426950408889634
QSCALE = SCALE * LOG2E

LANES = 128
VMEM_LIMIT = 56 * 1024 * 1024

TM_PROJ = 512
TM_FFN = 1024
TQ = 256
TQ_C = 256
SUB = 512
TKB_MAX = 2048
TF = 256
HALO = 16
LAZY_MARGIN = 20.0


def _cparams(n_axes):
    return pltpu.CompilerParams(
        dimension_semantics=("arbitrary",) * n_axes,
        vmem_limit_bytes=VMEM_LIMIT)


def _rms(x, g):
    ms = jnp.mean(x * x, axis=-1, keepdims=True)
    return x * lax.rsqrt(ms + EPS) * g


def _lane_iota(shape):
    return lax.broadcasted_iota(jnp.int32, shape, len(shape) - 1)


def _rope(t, cos, sin):
    w = t.shape[1]
    lane = _lane_iota(t.shape)
    fwd = pltpu.roll(t, 32, 1)
    bwd = pltpu.roll(t, w - 32, 1)
    rot = jnp.where((lane % HEAD_DIM) < (HEAD_DIM // 2), bwd, fwd)
    return t * cos + rot * sin


def _tile_lanes(t, reps):
    return t if reps == 1 else jnp.concatenate([t] * reps, axis=1)


def _head_pieces(p, sw=None):
    lane = _lane_iota(p.shape)
    lo = lane < HEAD_DIM
    q = p if sw is None else sw
    even = jnp.where(lane == HEAD_DIM, 1.0, jnp.where(lo, p, 0.0))
    odd = jnp.where(lane == 0, 1.0, jnp.where(lo, 0.0, q))
    return even, odd


def _expand_heads(t, ones_lane=False):
    pieces = []
    for j in range(t.shape[1] // LANES):
        p = t[:, j * LANES:(j + 1) * LANES]
        if ones_lane:
            pieces.extend(_head_pieces(p))
        else:
            lo = _lane_iota(p.shape) < HEAD_DIM
            pieces.append(jnp.where(lo, p, 0.0))
            pieces.append(jnp.where(lo, 0.0, p))
    return jnp.concatenate(pieces, axis=1).astype(BF16)


def _expand_kv_pair(t, ones_lane=False):
    lo = _lane_iota(t.shape) < HEAD_DIM
    sw = pltpu.roll(t, HEAD_DIM, 1)
    if ones_lane:
        e0, o0 = _head_pieces(t, sw)
        e1, o1 = _head_pieces(sw, t)
        pieces = [e0, o0, e1, o1]
    else:
        pieces = [jnp.where(lo, t, 0.0), jnp.where(lo, 0.0, sw),
                  jnp.where(lo, sw, 0.0), jnp.where(lo, 0.0, t)]
    return jnp.concatenate(pieces, axis=1).astype(BF16)


def _dot(a, b):
    return jnp.dot(a, b, preferred_element_type=F32)


def _dot_nt(a, b):
    return lax.dot_general(a, b, (((1,), (1,)), ((), ())), preferred_element_type=F32)


def _even_proj_kernel(x_ref, g_ref, w_ref, cos_ref, sin_ref, lng_ref, lnb_ref,
                      qa_ref, ka_ref, va_ref, qi_ref, ki_ref, wi_ref,
                      qb_ref, kb_ref, vb_ref):
    xn = _rms(x_ref[...], g_ref[...]).astype(BF16)
    cos = cos_ref[...]
    sin = sin_ref[...]
    cos4 = _tile_lanes(cos, 4)
    sin4 = _tile_lanes(sin, 4)

    def proj(a, b):
        return _dot(xn, w_ref[:, a:b])

    qa_ref[...] = (_rope(proj(0, 512), cos4, sin4) * QSCALE).astype(BF16)
    ka_ref[...] = _expand_heads(_rope(proj(512, 1024), cos4, sin4))
    va_ref[...] = _expand_heads(proj(1024, 1536), ones_lane=True)
    qi_ref[...] = _rope(proj(1536, 1792), _tile_lanes(cos, 2), _tile_lanes(sin, 2)).astype(BF16)
    small = proj(1792, 2048)
    kraw = small[:, :LANES]
    wi_ref[...] = small[:, LANES:]
    valid = _lane_iota(kraw.shape) < IDX_DIM
    mean = jnp.sum(kraw, axis=-1, keepdims=True) * (1.0 / IDX_DIM)
    xc = jnp.where(valid, kraw - mean, 0.0)
    var = jnp.sum(xc * xc, axis=-1, keepdims=True) * (1.0 / IDX_DIM)
    kn = xc * lax.rsqrt(var + EPS) * lng_ref[...] + lnb_ref[...]
    ki_ref[...] = _rope(kn, cos, sin).astype(BF16)
    qb_ref[...] = (proj(2048, 2560) * QSCALE).astype(BF16)
    kb_ref[...] = _expand_heads(proj(2560, 3072))
    vb_ref[...] = _expand_heads(proj(3072, 3584), ones_lane=True)


def _even_proj(x2, g, w, cos, sin, lng, lnb, seq):
    m = x2.shape[0]
    tm = min(TM_PROJ, seq)
    nseq = seq // tm
    row = lambda i: (i, 0)
    const = lambda i: (0, 0)
    tab = lambda i: (i % nseq, 0)
    widths = [(512, BF16), (1024, BF16), (1024, BF16), (256, BF16), (128, BF16),
              (128, F32), (512, BF16), (1024, BF16), (1024, BF16)]
    return pl.pallas_call(
        _even_proj_kernel,
        grid=(m // tm,),
        in_specs=[pl.BlockSpec((tm, D_MODEL), row),
                  pl.BlockSpec((1, D_MODEL), const),
                  pl.BlockSpec(w.shape, const),
                  pl.BlockSpec((tm, LANES), tab),
                  pl.BlockSpec((tm, LANES), tab),
                  pl.BlockSpec((1, LANES), const),
                  pl.BlockSpec((1, LANES), const)],
        out_specs=[pl.BlockSpec((tm, wd), row) for wd, _ in widths],
        out_shape=[jax.ShapeDtypeStruct((m, wd), dt) for wd, dt in widths],
        compiler_params=_cparams(1),
        name="even_proj",
    )(x2, g, w, cos, sin, lng, lnb)


def _odd_proj_kernel(x_ref, g_ref, w_ref, cos_ref, sin_ref, fb_ref, tril_ref,
                     qc_ref, kc_ref, vc_ref, f_ref, qd_ref, kd_ref, vd_ref,
                     carry_ref, *, nseq):
    i = pl.program_id(0)
    xn = _rms(x_ref[...], g_ref[...]).astype(BF16)
    cos = cos_ref[...]
    sin = sin_ref[...]

    def proj(a, b):
        return _dot(xn, w_ref[:, a:b])

    qc_ref[...] = (proj(0, 512) * QSCALE).astype(BF16)
    kc_ref[...] = _expand_heads(proj(512, 1024))
    vc_ref[...] = _expand_heads(proj(1024, 1536), ones_lane=True)

    z = proj(1536, 1664) + fb_ref[...]
    lf = jnp.minimum(z, 0.0) - jnp.log(1.0 + jnp.exp(-jnp.abs(z)))
    lf = jnp.where(_lane_iota(lf.shape) < N_HEADS, lf, 0.0)
    hi = lf.astype(BF16)
    r1 = lf - hi.astype(F32)
    mid = r1.astype(BF16)
    lo = (r1 - mid.astype(F32)).astype(BF16)
    tril = tril_ref[...]
    csum = _dot(tril, hi) + _dot(tril, mid) + _dot(tril, lo)

    @pl.when(i % nseq == 0)
    def _():
        carry_ref[...] = jnp.zeros_like(carry_ref)

    fcum = csum + carry_ref[0:1, :]
    f_ref[...] = fcum * LOG2E
    carry_ref[...] = jnp.broadcast_to(fcum[fcum.shape[0] - 1:, :], carry_ref.shape)

    qd_ref[...] = (_rope(proj(1664, 2176), _tile_lanes(cos, 4), _tile_lanes(sin, 4))
                   * QSCALE).astype(BF16)
    kv = proj(2176, 2432)
    kd_ref[...] = _expand_kv_pair(_rope(kv[:, :LANES], cos, sin))
    vd_ref[...] = _expand_kv_pair(kv[:, LANES:], ones_lane=True)


def _odd_proj(x2, g, w, cos, sin, fb, seq):
    m = x2.shape[0]
    tm = min(TM_PROJ, seq)
    nseq = seq // tm
    row = lambda i: (i, 0)
    const = lambda i: (0, 0)
    tab = lambda i: (i % nseq, 0)
    tril = jnp.tril(jnp.ones((tm, tm), F32)).astype(BF16)
    widths = [(512, BF16), (1024, BF16), (1024, BF16), (128, F32),
              (512, BF16), (512, BF16), (512, BF16)]
    return pl.pallas_call(
        functools.partial(_odd_proj_kernel, nseq=nseq),
        grid=(m // tm,),
        in_specs=[pl.BlockSpec((tm, D_MODEL), row),
                  pl.BlockSpec((1, D_MODEL), const),
                  pl.BlockSpec(w.shape, const),
                  pl.BlockSpec((tm, LANES), tab),
                  pl.BlockSpec((tm, LANES), tab),
                  pl.BlockSpec((1, LANES), const),
                  pl.BlockSpec((tm, tm), const)],
        out_specs=[pl.BlockSpec((tm, wd), row) for wd, _ in widths],
        out_shape=[jax.ShapeDtypeStruct((m, wd), dt) for wd, dt in widths],
        scratch_shapes=[pltpu.VMEM((8, LANES), F32)],
        compiler_params=_cparams(1),
        name="odd_proj",
    )(x2, g, w, cos, sin, fb, tril)


def _denominator_lane(h):
    return HEAD_DIM if h % 2 == 0 else 0


def _flash_heads(q_ref, k_ref, v_ref, rows, acc_ref, m_ref, logit_fn):
    nrep = SUB // LANES
    s_next = _head_scores(q_ref, k_ref, rows, 0)
    for h in range(N_HEADS):
        s_raw = s_next
        if h + 1 < N_HEADS:
            s_next = _head_scores(q_ref, k_ref, rows, h + 1)
        s, shift = logit_fn(h, s_raw)
        m_prev = m_ref[h]
        m_cur = jnp.max(s, axis=1, keepdims=True)
        if shift is not None:
            m_cur = m_cur + shift
        m_new = jnp.maximum(m_prev, m_cur)
        alpha = jnp.exp2(m_prev - m_new)
        off = m_new if shift is None else m_new - shift
        p = jnp.exp2(s - _tile_lanes(off, nrep))
        m_ref[h] = m_new
        cols = slice(h * LANES, (h + 1) * LANES)
        acc_ref[:, cols] = acc_ref[:, cols] * alpha + _dot(p.astype(BF16), v_ref[rows, cols])


def _head_scores(q_ref, k_ref, rows, h):
    jp = h // 2
    return _dot_nt(q_ref[:, jp * LANES:(jp + 1) * LANES], k_ref[rows, h * LANES:(h + 1) * LANES])


def _flash_heads_lazy(q_ref, k_ref, v_ref, rows, acc_ref, m_ref, tmp_ref, logit_fn):
    nrep = SUB // LANES
    worst = None
    pending = [_head_scores(q_ref, k_ref, rows, h) for h in range(2)]
    for h in range(N_HEADS):
        if h + 2 < N_HEADS:
            pending.append(_head_scores(q_ref, k_ref, rows, h + 2))
        s, shift = logit_fn(h, pending.pop(0))
        ref = m_ref[h] if shift is None else m_ref[h] - shift
        lane_max = functools.reduce(
            jnp.maximum, [s[:, t * LANES:(t + 1) * LANES] for t in range(nrep)])
        over = lane_max - ref
        worst = over if worst is None else jnp.maximum(worst, over)
        p = jnp.exp2(s - _tile_lanes(ref, nrep))
        cols = slice(h * LANES, (h + 1) * LANES)
        tmp_ref[:, cols] = _dot(p.astype(BF16), v_ref[rows, cols])
    exceeded = jnp.max(worst) > LAZY_MARGIN

    @pl.when(jnp.logical_not(exceeded))
    def _():
        acc_ref[...] += tmp_ref[...]

    @pl.when(exceeded)
    def _():
        _flash_heads(q_ref, k_ref, v_ref, rows, acc_ref, m_ref, logit_fn)


def _flash_init(acc_ref, m_ref):
    acc_ref[...] = jnp.zeros_like(acc_ref)
    m_ref[...] = jnp.full(m_ref.shape, NEG, F32)


def _pair_output(even, odd, l_even, l_odd):
    lo = _lane_iota(even.shape) < HEAD_DIM
    return jnp.where(lo, even / l_even, odd / l_odd)


def _flash_finish(o_ref, acc_ref):
    for jp in range(N_HEADS // 2):
        even = acc_ref[:, (2 * jp) * LANES:(2 * jp + 1) * LANES]
        odd = acc_ref[:, (2 * jp + 1) * LANES:(2 * jp + 2) * LANES]
        le = even[:, _denominator_lane(0):_denominator_lane(0) + 1]
        lo_ = odd[:, _denominator_lane(1):_denominator_lane(1) + 1]
        o_ref[:, jp * LANES:(jp + 1) * LANES] = _pair_output(even, odd, le, lo_).astype(o_ref.dtype)


def _kv_tiles(seq):
    tkb = min(TKB_MAX, seq)
    return tkb, seq // tkb, tkb // SUB


def _mixer_c_kernel(q_ref, k_ref, v_ref, fq_ref, fk_ref, o_ref, acc_ref, m_ref, fqr_ref,
                    *, tq, tkb, nsub):
    i = pl.program_id(1)
    j = pl.program_id(2)
    q_lo = i * tq
    last_sub = (q_lo + tq - 1) // SUB
    first_masked = q_lo // SUB
    jmax = last_sub // nsub

    @pl.when(j == 0)
    def _():
        _flash_init(acc_ref, m_ref)
        for h in range(N_HEADS):
            fqr_ref[h] = jnp.broadcast_to(fq_ref[:, h:h + 1], (tq, LANES))

    def step(c, masked):
        rows = pl.ds(pl.multiple_of(c * SUB, SUB), SUB)

        def logit_fn(h, s):
            s = s - fk_ref[h:h + 1, rows]
            if masked:
                kpos = j * tkb + c * SUB + lax.broadcasted_iota(jnp.int32, s.shape, 1)
                qpos = q_lo + lax.broadcasted_iota(jnp.int32, s.shape, 0)
                s = jnp.where(kpos <= qpos, s, NEG)
            return s, fqr_ref[h]

        _flash_heads(q_ref, k_ref, v_ref, rows, acc_ref, m_ref, logit_fn)

    @pl.when(j <= jmax)
    def _():
        lo = j * nsub
        n_full = jnp.clip(first_masked - lo, 0, nsub)
        n_all = jnp.clip(last_sub + 1 - lo, 0, nsub)

        def body(c, carry):
            step(c, False)
            return carry

        def masked_body(c, carry):
            step(c, True)
            return carry

        lax.fori_loop(0, n_full, body, 0)
        lax.fori_loop(n_full, n_all, masked_body, 0)

        @pl.when(j == jmax)
        def _():
            _flash_finish(o_ref, acc_ref)


def _mixer_c(q, k, v, fq, fk):
    bsz, seq, _ = q.shape
    tq = min(TQ_C, seq)
    tkb, nkb, nsub = _kv_tiles(seq)
    kv_idx = lambda b, i, j: (b, jnp.minimum(j, (i * tq + tq - 1) // tkb), 0)
    return pl.pallas_call(
        functools.partial(_mixer_c_kernel, tq=tq, tkb=tkb, nsub=nsub),
        grid=(bsz, seq // tq, nkb),
        in_specs=[pl.BlockSpec((None, tq, HW), lambda b, i, j: (b, i, 0)),
                  pl.BlockSpec((None, tkb, 2 * HW), kv_idx),
                  pl.BlockSpec((None, tkb, 2 * HW), kv_idx),
                  pl.BlockSpec((None, tq, LANES), lambda b, i, j: (b, i, 0)),
                  pl.BlockSpec((None, N_HEADS, tkb),
                               lambda b, i, j: (b, 0, jnp.minimum(j, (i * tq + tq - 1) // tkb)))],
        out_specs=pl.BlockSpec((None, tq, HW), lambda b, i, j: (b, i, 0)),
        out_shape=jax.ShapeDtypeStruct((bsz, seq, HW), BF16),
        scratch_shapes=[pltpu.VMEM((tq, 2 * HW), F32),
                        pltpu.VMEM((N_HEADS, tq, LANES), F32),
                        pltpu.VMEM((N_HEADS, tq, LANES), F32)],
        compiler_params=_cparams(3),
        name="mixer_c",
    )(q, k, v, fq, fk)


MIN_NORMAL_KEY = 0x00800000
BLIND_PASSES = 10


def _key_flip(bits):
    return jnp.where(bits >= 0, bits, bits ^ jnp.int32(0x7FFFFFFF))


def _float_to_key(f):
    return _key_flip(pltpu.bitcast(f, jnp.int32))


def _key_to_float(k):
    return pltpu.bitcast(_key_flip(k), F32)


def _mixer_a_kernel(qi_ref, wi_ref, kie_ref, kio_ref, q_ref, k_ref, v_ref, tri_ref,
                    o_ref, s_ref, gmax_ref, cnt_ref, acc_ref, m_ref, tmp_ref,
                    *, tkb, nsub, topk):
    i = pl.program_id(1)
    j = pl.program_id(2)
    q_lo = i * TQ
    last_sub = q_lo // SUB
    jmax = last_sub // nsub
    n_sub = last_sub + 1
    nrep = SUB // LANES
    half = TQ // 2

    def admissible(c_glob, shape):
        kpos = c_glob * SUB + lax.broadcasted_iota(jnp.int32, shape, 1)
        qpos = q_lo + lax.broadcasted_iota(jnp.int32, shape, 0)
        return (kpos // CHUNK) <= (qpos // CHUNK)

    def s_load(c):
        return jnp.concatenate([s_ref[c * nrep + t] for t in range(nrep)], axis=1)

    def s_store(c, val):
        for t in range(nrep):
            s_ref[c * nrep + t] = val[:, t * LANES:(t + 1) * LANES]

    def masked_logits(c_glob):
        return lambda h, s: (s + s_load(c_glob), None)

    @pl.when(j == 0)
    def _():
        _flash_init(acc_ref, m_ref)
        cnt_ref[...] = jnp.zeros_like(cnt_ref)
        gmax_ref[...] = jnp.full(gmax_ref.shape, -jnp.inf, F32)

        def score(c, masked):
            cols = pl.ds(pl.multiple_of(c * SUB, SUB), SUB)
            tot = None
            for pair in range(IDX_HEADS // 2):
                qp = qi_ref[:, pair * LANES:(pair + 1) * LANES]
                for e in range(2):
                    h = 2 * pair + e
                    kt = (kie_ref if e == 0 else kio_ref)[:, cols]
                    term = jnp.maximum(_dot(qp, kt), 0.0) * wi_ref[:, h:h + 1]
                    tot = term if tot is None else tot + term
            if masked:
                tot = jnp.where(admissible(c, tot.shape), tot, NEG)
            s_store(c, tot)
            for par in range(2):
                g = gmax_ref[par]
                for t in range(par, nrep, 2):
                    g = jnp.maximum(g, tot[:, t * LANES:(t + 1) * LANES])
                gmax_ref[par] = g

        def score_body(c, carry):
            score(c, False)
            return carry

        lax.fori_loop(0, last_sub, score_body, 0)
        score(last_sub, True)

        def count(cand, strict):
            accs = []
            for hh in range(2):
                r0 = hh * half
                ch = cand[r0:r0 + half]

                def body(c, acc):
                    for t in range(nrep):
                        blk = s_ref[c * nrep + t, r0:r0 + half, :]
                        hit = (blk > ch) if strict else (blk >= ch)
                        acc = acc + jnp.where(hit, 1.0, 0.0)
                    return acc

                accs.append(lax.fori_loop(0, n_sub, body, jnp.zeros((half, LANES), F32)))
            total = jnp.sum(jnp.concatenate(accs, axis=0), axis=1, keepdims=True)
            return jnp.broadcast_to(total, (TQ, LANES))

        ga = gmax_ref[0]
        gb = gmax_ref[1]
        lo_f = jnp.min(jnp.minimum(ga, gb), axis=1, keepdims=True)
        hi_f = jnp.max(jnp.maximum(ga, gb), axis=1, keepdims=True)
        lo0 = jnp.broadcast_to(_float_to_key(lo_f), (TQ, LANES))
        hi0 = jnp.broadcast_to(_float_to_key(hi_f), (TQ, LANES))

        def unresolved(st):
            lo, hi = st
            return jnp.max(jnp.where(lo < hi, 1.0, 0.0)) > 0.5

        def narrow(lo, hi, mid):
            n_ge = count(_key_to_float(mid), False)
            ge = n_ge >= topk
            below = jnp.where(mid == MIN_NORMAL_KEY, 0, mid - 1)
            return (jnp.where(ge, mid, lo),
                    jnp.where(n_ge == topk, mid, jnp.where(ge, hi, below)))

        def bisect(st):
            lo, hi = st
            by_key = (lo >> 1) + (hi >> 1) + ((lo | hi) & 1)
            lo_f = _key_to_float(lo)
            hi_f = _key_to_float(hi)
            by_val = _float_to_key(lo_f + (hi_f - lo_f) * 0.5)
            by_val = jnp.minimum(jnp.maximum(by_val, lo + 1), hi)
            same_binade = (pltpu.bitcast(lo_f, jnp.int32) >> 23) == (pltpu.bitcast(hi_f, jnp.int32) >> 23)
            mid = jnp.where(same_binade, by_key, by_val)
            mid = jnp.where(lo < 0, jnp.where(hi >= 0, 0, mid), mid)
            mid = jnp.where(lo == 0, jnp.where(hi >= MIN_NORMAL_KEY, MIN_NORMAL_KEY, mid), mid)
            mid = jnp.where(lo < hi, mid, lo)
            return narrow(lo, hi, mid)

        state = narrow(lo0, hi0, hi0)
        state = lax.fori_loop(0, BLIND_PASSES, lambda _, st: bisect(st), state)
        lo, _ = lax.while_loop(unresolved, bisect, state)
        tau1 = _key_to_float(lo)
        keep1 = topk - count(tau1, True)
        tau = _tile_lanes(tau1, nrep)
        keep = _tile_lanes(keep1, nrep)

        def to_mask(c, masked):
            blk = s_load(c)
            eq = blk == tau
            rank = (_tile_lanes(cnt_ref[...], nrep)
                    + _dot(jnp.where(eq, 1.0, 0.0).astype(BF16), tri_ref[...]))
            keep_tie = jnp.where(rank <= keep, 0.0, NEG)
            bias = jnp.where(blk > tau, 0.0, jnp.where(eq, keep_tie, NEG))
            if masked:
                bias = jnp.where(admissible(c, bias.shape), bias, NEG)
            cnt_ref[...] = jnp.broadcast_to(rank[:, SUB - 1:SUB], (TQ, LANES))
            s_store(c, bias)

        def mask_body(c, carry):
            to_mask(c, False)
            return carry

        lax.fori_loop(0, last_sub, mask_body, 0)
        to_mask(last_sub, True)

        _flash_heads(q_ref, k_ref, v_ref, pl.ds(0, SUB), acc_ref, m_ref, masked_logits(0))

    @pl.when(j <= jmax)
    def _():
        def body(c, carry):
            rows = pl.ds(pl.multiple_of(c * SUB, SUB), SUB)
            _flash_heads_lazy(q_ref, k_ref, v_ref, rows, acc_ref, m_ref, tmp_ref,
                              masked_logits(j * nsub + c))
            return carry

        lax.fori_loop(jnp.where(j == 0, 1, 0), jnp.minimum(n_sub - j * nsub, nsub), body, 0)

        @pl.when(j == jmax)
        def _():
            _flash_finish(o_ref, acc_ref)


def _mixer_a(qi, wi, kie, kio, q, k, v):
    bsz, seq, _ = q.shape
    tkb, nkb, nsub = _kv_tiles(seq)
    topk = min(IDX_TOPK_MAX, seq // 4)
    tri = jnp.triu(jnp.ones((SUB, SUB), F32)).astype(BF16)
    qmap = lambda b, i, j: (b, i, 0)
    kv_idx = lambda b, i, j: (b, jnp.minimum(j, (i * TQ) // tkb), 0)
    res = lambda b, i, j: (b, 0, 0)
    return pl.pallas_call(
        functools.partial(_mixer_a_kernel, tkb=tkb, nsub=nsub, topk=topk),
        grid=(bsz, seq // TQ, nkb),
        in_specs=[pl.BlockSpec((None, TQ, IDX_HEADS * IDX_DIM), qmap),
                  pl.BlockSpec((None, TQ, LANES), qmap),
                  pl.BlockSpec((None, LANES, seq), res),
                  pl.BlockSpec((None, LANES, seq), res),
                  pl.BlockSpec((None, TQ, HW), qmap),
                  pl.BlockSpec((None, tkb, 2 * HW), kv_idx),
                  pl.BlockSpec((None, tkb, 2 * HW), kv_idx),
                  pl.BlockSpec((SUB, SUB), lambda b, i, j: (0, 0))],
        out_specs=pl.BlockSpec((None, TQ, HW), qmap),
        out_shape=jax.ShapeDtypeStruct((bsz, seq, HW), BF16),
        scratch_shapes=[pltpu.VMEM((seq // LANES, TQ, LANES), F32),
                        pltpu.VMEM((2, TQ, LANES), F32),
                        pltpu.VMEM((TQ, LANES), F32),
                        pltpu.VMEM((TQ, 2 * HW), F32),
                        pltpu.VMEM((N_HEADS, TQ, LANES), F32),
                        pltpu.VMEM((TQ, 2 * HW), F32)],
        compiler_params=_cparams(3),
        name="mixer_a",
    )(qi, wi, kie, kio, q, k, v, tri)


B_WIN = 3


def _window_heads(q_ref, o_ref, n_win, k_of, v_of, bias_of, sink_of=None):
    def scores(h):
        jp = h // 2
        qp = q_ref[:, jp * LANES:(jp + 1) * LANES]
        return [_dot_nt(qp, k_of(h, a)) + bias_of(h, a) for a in range(n_win)]

    s_next = scores(0)
    done = []
    for h in range(N_HEADS):
        s = s_next
        if h + 1 < N_HEADS:
            s_next = scores(h + 1)
        m = functools.reduce(jnp.maximum, [jnp.max(t, axis=1, keepdims=True) for t in s])
        if sink_of is not None:
            m = jnp.maximum(m, sink_of(h))
        pv = functools.reduce(
            jnp.add, [_dot(jnp.exp2(s[a] - m).astype(BF16), v_of(h, a)) for a in range(n_win)])
        lane = _denominator_lane(h)
        l = pv[:, lane:lane + 1]
        if sink_of is not None:
            l = l + jnp.exp2(sink_of(h) - m)
        done.append((pv, l))
        if h % 2 == 1:
            (pe, le), (po, lo_) = done[-2], done[-1]
            jp = h // 2
            o_ref[:, jp * LANES:(jp + 1) * LANES] = _pair_output(pe, po, le, lo_).astype(o_ref.dtype)


def _mixer_b_kernel(q_ref, k0_ref, k1_ref, k2_ref, v0_ref, v1_ref, v2_ref, bias_ref, o_ref):
    i = pl.program_id(1)
    k_refs = (k0_ref, k1_ref, k2_ref)
    v_refs = (v0_ref, v1_ref, v2_ref)
    pad = [jnp.where(i - (B_WIN - 1 - a) >= 0, 0.0, NEG) for a in range(B_WIN)]
    hc = lambda h: slice(h * LANES, (h + 1) * LANES)
    _window_heads(q_ref, o_ref, B_WIN,
                  lambda h, a: k_refs[a][:, hc(h)],
                  lambda h, a: v_refs[a][:, hc(h)],
                  lambda h, a: bias_ref[h, :, a * TQ:(a + 1) * TQ] + pad[a])


def _mixer_b(q, k, v, bias_tab):
    bsz, seq, _ = q.shape
    qmap = lambda b, i: (b, i, 0)
    kmaps = [functools.partial(lambda b, i, d: (b, jnp.maximum(i - d, 0), 0), d=B_WIN - 1 - a)
             for a in range(B_WIN)]
    kspec = [pl.BlockSpec((None, TQ, 2 * HW), km) for km in kmaps]
    return pl.pallas_call(
        _mixer_b_kernel,
        grid=(bsz, seq // TQ),
        in_specs=[pl.BlockSpec((None, TQ, HW), qmap)] + kspec + kspec
                 + [pl.BlockSpec(bias_tab.shape, lambda b, i: (0, 0, 0))],
        out_specs=pl.BlockSpec((None, TQ, HW), qmap),
        out_shape=jax.ShapeDtypeStruct((bsz, seq, HW), BF16),
        compiler_params=_cparams(2),
        name="mixer_b",
    )(q, k, k, k, v, v, v, bias_tab)


def _b_bias_table(rel_bias):
    win = B_WIN * TQ
    span = TQ + win
    dist = (win - 1) - np.arange(span)
    idx = np.clip(dist, -B_REL_CLIP, B_REL_CLIP) + B_REL_CLIP
    v = rel_bias.astype(F32)[:, idx] * LOG2E
    skew = jnp.tile(v, (1, TQ + 1))[:, :TQ * (span + 1)].reshape(N_HEADS, TQ, span + 1)
    tab = skew[:, ::-1, :win]
    t = np.arange(TQ)[:, None]
    jk = np.arange(win)[None, :]
    cdiff = (t // CHUNK + (B_WIN - 1) * TQ // CHUNK) - jk // CHUNK
    ok = (cdiff >= 0) & (cdiff <= B_PAST_CHUNKS)
    return jnp.where(jnp.asarray(ok)[None], tab, NEG)


def _mixer_d_kernel(q_ref, k0_ref, k1_ref, v0_ref, v1_ref, sink_ref, o_ref):
    i = pl.program_id(1)
    k_refs = (k0_ref, k1_ref)
    v_refs = (v0_ref, v1_ref)
    qpos = TQ + lax.broadcasted_iota(jnp.int32, (TQ, TQ), 0)
    bias = []
    for a in range(2):
        kpos = a * TQ + lax.broadcasted_iota(jnp.int32, (TQ, TQ), 1)
        cdiff = qpos // CHUNK - kpos // CHUNK
        ok = (cdiff >= 0) & (cdiff <= D_WINDOW_CHUNKS)
        if a == 0:
            ok = ok & (i > 0)
        bias.append(jnp.where(ok, 0.0, NEG))
    group = N_HEADS // D_KV_HEADS

    def piece(h):
        n = (h // group) * 2 + h % 2
        return slice(n * LANES, (n + 1) * LANES)

    _window_heads(q_ref, o_ref, 2,
                  lambda h, a: k_refs[a][:, piece(h)],
                  lambda h, a: v_refs[a][:, piece(h)],
                  lambda h, a: bias[a],
                  sink_of=lambda h: sink_ref[h:h + 1, 0:1])


def _mixer_d(q, k, v, sinks):
    bsz, seq, _ = q.shape
    qmap = lambda b, i: (b, i, 0)
    prev = lambda b, i: (b, jnp.maximum(i - 1, 0), 0)
    sink_tab = jnp.broadcast_to((sinks.astype(F32) * LOG2E)[:, None], (N_HEADS, LANES))
    return pl.pallas_call(
        _mixer_d_kernel,
        grid=(bsz, seq // TQ),
        in_specs=[pl.BlockSpec((None, TQ, HW), qmap),
                  pl.BlockSpec((None, TQ, HW), prev), pl.BlockSpec((None, TQ, HW), qmap),
                  pl.BlockSpec((None, TQ, HW), prev), pl.BlockSpec((None, TQ, HW), qmap),
                  pl.BlockSpec((N_HEADS, LANES), lambda b, i: (0, 0))],
        out_specs=pl.BlockSpec((None, TQ, HW), qmap),
        out_shape=jax.ShapeDtypeStruct((bsz, seq, HW), BF16),
        compiler_params=_cparams(2),
        name="mixer_d",
    )(q, k, k, v, v, sink_tab)


def _out_proj_kernel(h_ref, y0_ref, y1_ref, w0_ref, w1_ref, o_ref):
    o_ref[...] = h_ref[...] + _dot(y0_ref[...], w0_ref[...]) + _dot(y1_ref[...], w1_ref[...])


def _out_proj(h2, y0, y1, w0, w1, seq):
    m = h2.shape[0]
    tm = min(TM_PROJ, seq)
    row = lambda i: (i, 0)
    const = lambda i: (0, 0)
    return pl.pallas_call(
        _out_proj_kernel,
        grid=(m // tm,),
        in_specs=[pl.BlockSpec((tm, D_MODEL), row),
                  pl.BlockSpec((tm, HW), row), pl.BlockSpec((tm, HW), row),
                  pl.BlockSpec((HW, D_MODEL), const), pl.BlockSpec((HW, D_MODEL), const)],
        out_specs=pl.BlockSpec((tm, D_MODEL), row),
        out_shape=jax.ShapeDtypeStruct((m, D_MODEL), F32),
        compiler_params=_cparams(1),
        name="out_proj",
    )(h2, y0, y1, w0, w1)


def _ffn_kernel(h_ref, halo_ref, g_ref, wg_ref, wu_ref, cwg_ref, cwu_ref, cbg_ref, cbu_ref,
                wd_ref, go_ref, o_ref, xn_ref, xh_ref, acc_ref, hb_ref, *, nseq, final_norm):
    i = pl.program_id(0)
    f = pl.program_id(1)
    tm = h_ref.shape[0]

    @pl.when(f == 0)
    def _():
        xn_ref[...] = _rms(h_ref[...], g_ref[...]).astype(BF16)
        xh_ref[...] = _rms(halo_ref[...], g_ref[...]).astype(BF16)
        acc_ref[...] = jnp.zeros_like(acc_ref)

    halo_on = jnp.where(i % nseq == 0, 0.0, 1.0)

    def branch(w_ref, cw_ref, cb_ref):
        hb_ref[HALO:, :] = _dot(xn_ref[...], w_ref[...])
        hb_ref[:HALO, :] = _dot(xh_ref[...], w_ref[...]) * halo_on
        cw = cw_ref[...]
        return (hb_ref[pl.ds(HALO, tm), :] * cw[2:3, :]
                + hb_ref[pl.ds(HALO - 1, tm), :] * cw[1:2, :]
                + hb_ref[pl.ds(HALO - 2, tm), :] * cw[0:1, :]
                + cb_ref[...])

    yg = branch(wg_ref, cwg_ref, cbg_ref)
    yu = branch(wu_ref, cwu_ref, cbu_ref)
    act = (yg / (1.0 + jnp.exp(-yg))) * yu
    acc_ref[...] += _dot(act.astype(BF16), wd_ref[...])

    @pl.when(f == pl.num_programs(1) - 1)
    def _():
        out = h_ref[...] + acc_ref[...]
        if final_norm:
            out = _rms(out, go_ref[...])
        o_ref[...] = out


def _ffn(h2, g, w_up, conv_w, conv_b, w_down, g_out, seq, final_norm):
    m = h2.shape[0]
    tm = min(TM_FFN, seq)
    nseq = seq // tm
    nf = D_FF // TF
    row = lambda i, f: (i, 0)
    const = lambda i, f: (0, 0)
    halo = lambda i, f: (jnp.maximum(i * (tm // HALO) - 1, 0), 0)
    gcol = lambda i, f: (0, f)
    ucol = lambda i, f: (0, nf + f)
    return pl.pallas_call(
        functools.partial(_ffn_kernel, nseq=nseq, final_norm=final_norm),
        grid=(m // tm, nf),
        in_specs=[pl.BlockSpec((tm, D_MODEL), row),
                  pl.BlockSpec((HALO, D_MODEL), halo),
                  pl.BlockSpec((1, D_MODEL), const),
                  pl.BlockSpec((D_MODEL, TF), gcol), pl.BlockSpec((D_MODEL, TF), ucol),
                  pl.BlockSpec((3, TF), gcol), pl.BlockSpec((3, TF), ucol),
                  pl.BlockSpec((1, TF), gcol), pl.BlockSpec((1, TF), ucol),
                  pl.BlockSpec((TF, D_MODEL), lambda i, f: (f, 0)),
                  pl.BlockSpec((1, D_MODEL), const)],
        out_specs=pl.BlockSpec((tm, D_MODEL), row),
        out_shape=jax.ShapeDtypeStruct((m, D_MODEL), F32),
        scratch_shapes=[pltpu.VMEM((tm, D_MODEL), BF16),
                        pltpu.VMEM((HALO, D_MODEL), BF16),
                        pltpu.VMEM((tm, D_MODEL), F32),
                        pltpu.VMEM((tm + HALO, TF), F32)],
        compiler_params=_cparams(2),
        name="ffn",
    )(h2, h2, g, w_up, w_up, conv_w, conv_w, conv_b, conv_b, w_down, g_out)


def _rope_tables(seq):
    inv = 1.0 / (ROPE_THETA ** (jnp.arange(0, HEAD_DIM, 2, dtype=F32) / HEAD_DIM))
    ang = jnp.arange(seq, dtype=F32)[:, None] * inv[None, :]
    c, s = jnp.cos(ang), jnp.sin(ang)
    cos = jnp.concatenate([c, c, c, c], axis=1)
    sin = jnp.concatenate([-s, s, -s, s], axis=1)
    return cos, sin


def _pad_cols(w, width):
    return jnp.pad(w, ((0, 0), (0, width - w.shape[1])))


def _even_weights(w_in):
    o = np.cumsum([0, 512, 512, 512, IDX_HEADS * IDX_DIM, IDX_DIM, IDX_HEADS, 512, 512, 512])
    sl = [w_in[:, o[n]:o[n + 1]] for n in range(9)]
    small = jnp.concatenate([_pad_cols(sl[4], LANES), _pad_cols(sl[5], LANES)], axis=1)
    return jnp.concatenate([sl[0], sl[1], sl[2], sl[3], small, sl[6], sl[7], sl[8]],
                           axis=1).astype(BF16)


def _odd_weights(w_in):
    o = np.cumsum([0, 512, 512, 512, N_HEADS, 512, D_KV_HEADS * HEAD_DIM, D_KV_HEADS * HEAD_DIM])
    sl = [w_in[:, o[n]:o[n + 1]] for n in range(7)]
    return jnp.concatenate([sl[0], sl[1], sl[2], _pad_cols(sl[3], LANES), sl[4], sl[5], sl[6]],
                           axis=1).astype(BF16)


def _row(v, width=None):
    v = v.astype(F32)[None, :]
    return v if width is None else _pad_cols(v, width)


def kernel(x, norm_mix_g, norm_ffn_g, norm_out_g, even_w_in, even_w_out, idx_k_ln_g, idx_k_ln_b,
           rel_bias, odd_w_in, odd_w_out, forget_b, sinks, ffn_w_up, ffn_conv_w, ffn_conv_b,
           ffn_w_down):
    bsz, seq, _ = x.shape
    m = bsz * seq
    depth = norm_mix_g.shape[0]
    assert seq % TQ == 0 and seq % min(TM_PROJ, seq) == 0 and seq % min(TKB_MAX, seq) == 0
    cos, sin = _rope_tables(seq)
    h = x.reshape(m, D_MODEL)
    as3 = lambda t: t.reshape(bsz, seq, t.shape[-1])

    for layer in range(depth):
        jj = layer // 2
        g_mix = _row(norm_mix_g[layer])
        if layer % 2 == 0:
            qa, ka, va, qi, ki, wi, qb, kb, vb = _even_proj(
                h, g_mix, _even_weights(even_w_in[jj]), cos, sin,
                _row(idx_k_ln_g[jj], LANES), _row(idx_k_ln_b[jj], LANES), seq)
            kit = jnp.swapaxes(as3(ki), 1, 2)
            kie, kio = kit, jnp.roll(kit, IDX_DIM, axis=1)
            y0 = _mixer_a(as3(qi), as3(wi), kie, kio, as3(qa), as3(ka), as3(va))
            y1 = _mixer_b(as3(qb), as3(kb), as3(vb), _b_bias_table(rel_bias[jj]))
            w_out = even_w_out[jj]
        else:
            qc, kc, vc, fcum, qd, kd, vd = _odd_proj(
                h, g_mix, _odd_weights(odd_w_in[jj]), cos, sin,
                _row(forget_b[jj], LANES), seq)
            fq = as3(fcum)
            fk = jnp.swapaxes(fq[:, :, :N_HEADS], 1, 2)
            y0 = _mixer_c(as3(qc), as3(kc), as3(vc), fq, fk)
            y1 = _mixer_d(as3(qd), as3(kd), as3(vd), sinks[jj])
            w_out = odd_w_out[jj]
        w_out = w_out.astype(BF16)
        h = _out_proj(h, y0.reshape(m, HW), y1.reshape(m, HW), w_out[:HW], w_out[HW:], seq)
        h = _ffn(h, _row(norm_ffn_g[layer]), ffn_w_up[layer].astype(BF16),
                 ffn_conv_w[layer].astype(F32), _row(ffn_conv_b[layer]),
                 ffn_w_down[layer].astype(BF16), _row(norm_out_g), seq,
                 final_norm=(layer == depth - 1))
    return h.reshape(bsz, seq, D_MODEL)
```

```python
import functools

import numpy as np
import jax
import jax.numpy as jnp
from jax import lax
from jax.experimental import pallas as pl
from jax.experimental.pallas import tpu as pltpu

F32 = jnp.float32
BF16 = jnp.bfloat16

D_MODEL = 1024
HEAD_DIM = 64
CHUNK = 64
ROPE_THETA = 10000.0
EPS = 1e-6
NEG = -1e30
N_HEADS = 8
IDX_HEADS = 4
IDX_DIM = 64
IDX_TOPK_MAX = 256
B_PAST_CHUNKS = 8
B_REL_CLIP = 256
D_KV_HEADS = 2
D_WINDOW_CHUNKS = 2
D_FF = 2816
HW = N_HEADS * HEAD_DIM
SCALE = HEAD_DIM ** -0.5
LOG2E = 1.4426950408889634
QSCALE = SCALE * LOG2E

LANES = 128
VMEM_LIMIT = 56 * 1024 * 1024

TM_PROJ = 512
TM_FFN = 1024
TQ = 256
TQ_C = 256
SUB = 512
TKB_MAX = 2048
TKB_C = 4096
TF = 256
HALO = 16
LAZY_MARGIN = 20.0


def _cparams(n_axes):
    return pltpu.CompilerParams(
        dimension_semantics=("arbitrary",) * n_axes,
        vmem_limit_bytes=VMEM_LIMIT)


def _rms(x, g):
    ms = jnp.mean(x * x, axis=-1, keepdims=True)
    return x * lax.rsqrt(ms + EPS) * g


def _lane_iota(shape):
    return lax.broadcasted_iota(jnp.int32, shape, len(shape) - 1)


def _rope(t, cos, sin):
    w = t.shape[1]
    lane = _lane_iota(t.shape)
    fwd = pltpu.roll(t, 32, 1)
    bwd = pltpu.roll(t, w - 32, 1)
    rot = jnp.where((lane % HEAD_DIM) < (HEAD_DIM // 2), bwd, fwd)
    return t * cos + rot * sin


def _tile_lanes(t, reps):
    return t if reps == 1 else jnp.concatenate([t] * reps, axis=1)


def _head_pieces(p, sw=None):
    lane = _lane_iota(p.shape)
    lo = lane < HEAD_DIM
    q = p if sw is None else sw
    even = jnp.where(lane == HEAD_DIM, 1.0, jnp.where(lo, p, 0.0))
    odd = jnp.where(lane == 0, 1.0, jnp.where(lo, 0.0, q))
    return even, odd


def _expand_heads(t, ones_lane=False):
    pieces = []
    for j in range(t.shape[1] // LANES):
        p = t[:, j * LANES:(j + 1) * LANES]
        if ones_lane:
            pieces.extend(_head_pieces(p))
        else:
            lo = _lane_iota(p.shape) < HEAD_DIM
            pieces.append(jnp.where(lo, p, 0.0))
            pieces.append(jnp.where(lo, 0.0, p))
    return jnp.concatenate(pieces, axis=1).astype(BF16)


def _expand_kv_pair(t, ones_lane=False):
    lo = _lane_iota(t.shape) < HEAD_DIM
    sw = pltpu.roll(t, HEAD_DIM, 1)
    if ones_lane:
        e0, o0 = _head_pieces(t, sw)
        e1, o1 = _head_pieces(sw, t)
        pieces = [e0, o0, e1, o1]
    else:
        pieces = [jnp.where(lo, t, 0.0), jnp.where(lo, 0.0, sw),
                  jnp.where(lo, sw, 0.0), jnp.where(lo, 0.0, t)]
    return jnp.concatenate(pieces, axis=1).astype(BF16)


def _dot(a, b):
    return jnp.dot(a, b, preferred_element_type=F32)


def _dot_nt(a, b):
    return lax.dot_general(a, b, (((1,), (1,)), ((), ())), preferred_element_type=F32)


def _even_proj_kernel(x_ref, g_ref, w_ref, cos_ref, sin_ref, lng_ref, lnb_ref,
                      qa_ref, ka_ref, va_ref, qi_ref, ki_ref, wi_ref,
                      qb_ref, kb_ref, vb_ref):
    xn = _rms(x_ref[...], g_ref[...]).astype(BF16)
    cos = cos_ref[...]
    sin = sin_ref[...]
    cos4 = _tile_lanes(cos, 4)
    sin4 = _tile_lanes(sin, 4)

    def proj(a, b):
        return _dot(xn, w_ref[:, a:b])

    qa_ref[...] = (_rope(proj(0, 512), cos4, sin4) * QSCALE).astype(BF16)
    ka_ref[...] = _expand_heads(_rope(proj(512, 1024), cos4, sin4))
    va_ref[...] = _expand_heads(proj(1024, 1536), ones_lane=True)
    qi_ref[...] = _rope(proj(1536, 1792), _tile_lanes(cos, 2), _tile_lanes(sin, 2)).astype(BF16)
    small = proj(1792, 2048)
    kraw = small[:, :LANES]
    wi_ref[...] = small[:, LANES:]
    valid = _lane_iota(kraw.shape) < IDX_DIM
    mean = jnp.sum(kraw, axis=-1, keepdims=True) * (1.0 / IDX_DIM)
    xc = jnp.where(valid, kraw - mean, 0.0)
    var = jnp.sum(xc * xc, axis=-1, keepdims=True) * (1.0 / IDX_DIM)
    kn = xc * lax.rsqrt(var + EPS) * lng_ref[...] + lnb_ref[...]
    ki_ref[...] = _rope(kn, cos, sin).astype(BF16)
    qb_ref[...] = (proj(2048, 2560) * QSCALE).astype(BF16)
    kb_ref[...] = _expand_heads(proj(2560, 3072))
    vb_ref[...] = _expand_heads(proj(3072, 3584), ones_lane=True)


def _even_proj(x2, g, w, cos, sin, lng, lnb, seq):
    m = x2.shape[0]
    tm = min(TM_PROJ, seq)
    nseq = seq // tm
    row = lambda i: (i, 0)
    const = lambda i: (0, 0)
    tab = lambda i: (i % nseq, 0)
    widths = [(512, BF16), (1024, BF16), (1024, BF16), (256, BF16), (128, BF16),
              (128, F32), (512, BF16), (1024, BF16), (1024, BF16)]
    return pl.pallas_call(
        _even_proj_kernel,
        grid=(m // tm,),
        in_specs=[pl.BlockSpec((tm, D_MODEL), row),
                  pl.BlockSpec((1, D_MODEL), const),
                  pl.BlockSpec(w.shape, const),
                  pl.BlockSpec((tm, LANES), tab),
                  pl.BlockSpec((tm, LANES), tab),
                  pl.BlockSpec((1, LANES), const),
                  pl.BlockSpec((1, LANES), const)],
        out_specs=[pl.BlockSpec((tm, wd), row) for wd, _ in widths],
        out_shape=[jax.ShapeDtypeStruct((m, wd), dt) for wd, dt in widths],
        compiler_params=_cparams(1),
        name="even_proj",
    )(x2, g, w, cos, sin, lng, lnb)


def _odd_proj_kernel(x_ref, g_ref, w_ref, cos_ref, sin_ref, fb_ref, tril_ref,
                     qc_ref, kc_ref, vc_ref, f_ref, qd_ref, kd_ref, vd_ref,
                     carry_ref, *, nseq):
    i = pl.program_id(0)
    xn = _rms(x_ref[...], g_ref[...]).astype(BF16)
    cos = cos_ref[...]
    sin = sin_ref[...]

    def proj(a, b):
        return _dot(xn, w_ref[:, a:b])

    qc_ref[...] = (proj(0, 512) * QSCALE).astype(BF16)
    kc_ref[...] = _expand_heads(proj(512, 1024))
    vc_ref[...] = _expand_heads(proj(1024, 1536), ones_lane=True)

    z = proj(1536, 1664) + fb_ref[...]
    lf = jnp.minimum(z, 0.0) - jnp.log(1.0 + jnp.exp(-jnp.abs(z)))
    lf = jnp.where(_lane_iota(lf.shape) < N_HEADS, lf, 0.0)
    hi = lf.astype(BF16)
    r1 = lf - hi.astype(F32)
    mid = r1.astype(BF16)
    lo = (r1 - mid.astype(F32)).astype(BF16)
    tril = tril_ref[...]
    csum = _dot(tril, hi) + _dot(tril, mid) + _dot(tril, lo)

    @pl.when(i % nseq == 0)
    def _():
        carry_ref[...] = jnp.zeros_like(carry_ref)

    fcum = csum + carry_ref[0:1, :]
    f_ref[...] = fcum * LOG2E
    carry_ref[...] = jnp.broadcast_to(fcum[fcum.shape[0] - 1:, :], carry_ref.shape)

    qd_ref[...] = (_rope(proj(1664, 2176), _tile_lanes(cos, 4), _tile_lanes(sin, 4))
                   * QSCALE).astype(BF16)
    kv = proj(2176, 2432)
    kd_ref[...] = _expand_kv_pair(_rope(kv[:, :LANES], cos, sin))
    vd_ref[...] = _expand_kv_pair(kv[:, LANES:], ones_lane=True)


def _odd_proj(x2, g, w, cos, sin, fb, seq):
    m = x2.shape[0]
    tm = min(TM_PROJ, seq)
    nseq = seq // tm
    row = lambda i: (i, 0)
    const = lambda i: (0, 0)
    tab = lambda i: (i % nseq, 0)
    tril = jnp.tril(jnp.ones((tm, tm), F32)).astype(BF16)
    widths = [(512, BF16), (1024, BF16), (1024, BF16), (128, F32),
              (512, BF16), (512, BF16), (512, BF16)]
    return pl.pallas_call(
        functools.partial(_odd_proj_kernel, nseq=nseq),
        grid=(m // tm,),
        in_specs=[pl.BlockSpec((tm, D_MODEL), row),
                  pl.BlockSpec((1, D_MODEL), const),
                  pl.BlockSpec(w.shape, const),
                  pl.BlockSpec((tm, LANES), tab),
                  pl.BlockSpec((tm, LANES), tab),
                  pl.BlockSpec((1, LANES), const),
                  pl.BlockSpec((tm, tm), const)],
        out_specs=[pl.BlockSpec((tm, wd), row) for wd, _ in widths],
        out_shape=[jax.ShapeDtypeStruct((m, wd), dt) for wd, dt in widths],
        scratch_shapes=[pltpu.VMEM((8, LANES), F32)],
        compiler_params=_cparams(1),
        name="odd_proj",
    )(x2, g, w, cos, sin, fb, tril)


def _denominator_lane(h):
    return HEAD_DIM if h % 2 == 0 else 0


def _flash_heads(q_ref, k_ref, v_ref, rows, acc_ref, m_ref, logit_fn):
    nrep = SUB // LANES
    s_next = _head_scores(q_ref, k_ref, rows, 0)
    for h in range(N_HEADS):
        s_raw = s_next
        if h + 1 < N_HEADS:
            s_next = _head_scores(q_ref, k_ref, rows, h + 1)
        s, shift = logit_fn(h, s_raw)
        m_prev = m_ref[h]
        m_cur = jnp.max(s, axis=1, keepdims=True)
        if shift is not None:
            m_cur = m_cur + shift
        m_new = jnp.maximum(m_prev, m_cur)
        alpha = jnp.exp2(m_prev - m_new)
        off = m_new if shift is None else m_new - shift
        p = jnp.exp2(s - _tile_lanes(off, nrep))
        m_ref[h] = m_new
        cols = slice(h * LANES, (h + 1) * LANES)
        acc_ref[:, cols] = acc_ref[:, cols] * alpha + _dot(p.astype(BF16), v_ref[rows, cols])


def _head_scores(q_ref, k_ref, rows, h):
    jp = h // 2
    return _dot_nt(q_ref[:, jp * LANES:(jp + 1) * LANES], k_ref[rows, h * LANES:(h + 1) * LANES])


def _flash_heads_lazy(q_ref, k_ref, v_ref, rows, acc_ref, m_ref, tmp_ref, logit_fn):
    nrep = SUB // LANES
    worst = None
    pending = [_head_scores(q_ref, k_ref, rows, h) for h in range(2)]
    for h in range(N_HEADS):
        if h + 2 < N_HEADS:
            pending.append(_head_scores(q_ref, k_ref, rows, h + 2))
        s, shift = logit_fn(h, pending.pop(0))
        ref = m_ref[h] if shift is None else m_ref[h] - shift
        lane_max = functools.reduce(
            jnp.maximum, [s[:, t * LANES:(t + 1) * LANES] for t in range(nrep)])
        over = lane_max - ref
        worst = over if worst is None else jnp.maximum(worst, over)
        p = jnp.exp2(s - _tile_lanes(ref, nrep))
        cols = slice(h * LANES, (h + 1) * LANES)
        tmp_ref[:, cols] = _dot(p.astype(BF16), v_ref[rows, cols])
    exceeded = jnp.max(worst) > LAZY_MARGIN

    @pl.when(jnp.logical_not(exceeded))
    def _():
        acc_ref[...] += tmp_ref[...]

    @pl.when(exceeded)
    def _():
        _flash_heads(q_ref, k_ref, v_ref, rows, acc_ref, m_ref, logit_fn)


def _flash_init(acc_ref, m_ref):
    acc_ref[...] = jnp.zeros_like(acc_ref)
    m_ref[...] = jnp.full(m_ref.shape, NEG, F32)


def _pair_output(even, odd, l_even, l_odd):
    lo = _lane_iota(even.shape) < HEAD_DIM
    return jnp.where(lo, even / l_even, odd / l_odd)


def _flash_finish(o_ref, acc_ref):
    for jp in range(N_HEADS // 2):
        even = acc_ref[:, (2 * jp) * LANES:(2 * jp + 1) * LANES]
        odd = acc_ref[:, (2 * jp + 1) * LANES:(2 * jp + 2) * LANES]
        le = even[:, _denominator_lane(0):_denominator_lane(0) + 1]
        lo_ = odd[:, _denominator_lane(1):_denominator_lane(1) + 1]
        o_ref[:, jp * LANES:(jp + 1) * LANES] = _pair_output(even, odd, le, lo_).astype(o_ref.dtype)


def _kv_tiles(seq, tkb_max=None):
    tkb = min(TKB_MAX if tkb_max is None else tkb_max, seq)
    return tkb, seq // tkb, tkb // SUB


def _mixer_c_kernel(q_ref, k_ref, v_ref, fq_ref, fk_ref, o_ref, acc_ref, m_ref, fqr_ref,
                    *, tq, tkb, nsub):
    i = pl.program_id(1)
    j = pl.program_id(2)
    q_lo = i * tq
    last_sub = (q_lo + tq - 1) // SUB
    first_masked = q_lo // SUB
    jmax = last_sub // nsub

    @pl.when(j == 0)
    def _():
        _flash_init(acc_ref, m_ref)
        for h in range(N_HEADS):
            fqr_ref[h] = jnp.broadcast_to(fq_ref[:, h:h + 1], (tq, LANES))

    def step(c, masked):
        rows = pl.ds(pl.multiple_of(c * SUB, SUB), SUB)

        def logit_fn(h, s):
            s = s - fk_ref[h:h + 1, rows]
            if masked:
                kpos = j * tkb + c * SUB + lax.broadcasted_iota(jnp.int32, s.shape, 1)
                qpos = q_lo + lax.broadcasted_iota(jnp.int32, s.shape, 0)
                s = jnp.where(kpos <= qpos, s, NEG)
            return s, fqr_ref[h]

        _flash_heads(q_ref, k_ref, v_ref, rows, acc_ref, m_ref, logit_fn)

    @pl.when(j <= jmax)
    def _():
        lo = j * nsub
        n_full = jnp.clip(first_masked - lo, 0, nsub)
        n_all = jnp.clip(last_sub + 1 - lo, 0, nsub)

        def body(c, carry):
            step(c, False)
            return carry

        def masked_body(c, carry):
            step(c, True)
            return carry

        lax.fori_loop(0, n_full, body, 0)
        lax.fori_loop(n_full, n_all, masked_body, 0)

        @pl.when(j == jmax)
        def _():
            _flash_finish(o_ref, acc_ref)


def _mixer_c(q, k, v, fq, fk):
    bsz, seq, _ = q.shape
    tq = min(TQ_C, seq)
    tkb, nkb, nsub = _kv_tiles(seq, TKB_C)
    kv_idx = lambda b, i, j: (b, jnp.minimum(j, (i * tq + tq - 1) // tkb), 0)
    return pl.pallas_call(
        functools.partial(_mixer_c_kernel, tq=tq, tkb=tkb, nsub=nsub),
        grid=(bsz, seq // tq, nkb),
        in_specs=[pl.BlockSpec((None, tq, HW), lambda b, i, j: (b, i, 0)),
                  pl.BlockSpec((None, tkb, 2 * HW), kv_idx),
                  pl.BlockSpec((None, tkb, 2 * HW), kv_idx),
                  pl.BlockSpec((None, tq, LANES), lambda b, i, j: (b, i, 0)),
                  pl.BlockSpec((None, N_HEADS, tkb),
                               lambda b, i, j: (b, 0, jnp.minimum(j, (i * tq + tq - 1) // tkb)))],
        out_specs=pl.BlockSpec((None, tq, HW), lambda b, i, j: (b, i, 0)),
        out_shape=jax.ShapeDtypeStruct((bsz, seq, HW), BF16),
        scratch_shapes=[pltpu.VMEM((tq, 2 * HW), F32),
                        pltpu.VMEM((N_HEADS, tq, LANES), F32),
                        pltpu.VMEM((N_HEADS, tq, LANES), F32)],
        compiler_params=_cparams(3),
        name="mixer_c",
    )(q, k, v, fq, fk)


MIN_NORMAL_KEY = 0x00800000
BLIND_PASSES = 10
COUNT_UNROLL = 4


def _key_flip(bits):
    return jnp.where(bits >= 0, bits, bits ^ jnp.int32(0x7FFFFFFF))


def _float_to_key(f):
    return _key_flip(pltpu.bitcast(f, jnp.int32))


def _key_to_float(k):
    return pltpu.bitcast(_key_flip(k), F32)


def _mixer_a_kernel(qi_ref, wi_ref, kie_ref, kio_ref, q_ref, k_ref, v_ref, tri_ref,
                    o_ref, s_ref, gmax_ref, cnt_ref, acc_ref, m_ref, tmp_ref,
                    *, tkb, nsub, topk):
    i = pl.program_id(1)
    j = pl.program_id(2)
    q_lo = i * TQ
    last_sub = q_lo // SUB
    jmax = last_sub // nsub
    n_sub = last_sub + 1
    nrep = SUB // LANES
    half = TQ // 2

    def admissible(c_glob, shape):
        kpos = c_glob * SUB + lax.broadcasted_iota(jnp.int32, shape, 1)
        qpos = q_lo + lax.broadcasted_iota(jnp.int32, shape, 0)
        return (kpos // CHUNK) <= (qpos // CHUNK)

    def s_load(c):
        return jnp.concatenate([s_ref[c * nrep + t] for t in range(nrep)], axis=1)

    def s_store(c, val):
        for t in range(nrep):
            s_ref[c * nrep + t] = val[:, t * LANES:(t + 1) * LANES]

    def masked_logits(c_glob):
        return lambda h, s: (s + s_load(c_glob), None)

    @pl.when(j == 0)
    def _():
        _flash_init(acc_ref, m_ref)
        cnt_ref[...] = jnp.zeros_like(cnt_ref)
        gmax_ref[...] = jnp.full(gmax_ref.shape, -jnp.inf, F32)

        def score(c, masked):
            cols = pl.ds(pl.multiple_of(c * SUB, SUB), SUB)
            tot = None
            for pair in range(IDX_HEADS // 2):
                qp = qi_ref[:, pair * LANES:(pair + 1) * LANES]
                for e in range(2):
                    h = 2 * pair + e
                    kt = (kie_ref if e == 0 else kio_ref)[:, cols]
                    term = jnp.maximum(_dot(qp, kt), 0.0) * wi_ref[:, h:h + 1]
                    tot = term if tot is None else tot + term
            if masked:
                tot = jnp.where(admissible(c, tot.shape), tot, NEG)
            s_store(c, tot)
            for par in range(2):
                g = gmax_ref[par]
                for t in range(par, nrep, 2):
                    g = jnp.maximum(g, tot[:, t * LANES:(t + 1) * LANES])
                gmax_ref[par] = g

        def score_body(c, carry):
            score(c, False)
            return carry

        lax.fori_loop(0, last_sub, score_body, 0)
        score(last_sub, True)

        def count(cand, strict):
            accs = []
            for hh in range(2):
                r0 = hh * half
                ch = cand[r0:r0 + half]

                def scan(t0, width, acc):
                    for t in range(width):
                        blk = s_ref[t0 + t, r0:r0 + half, :]
                        hit = (blk > ch) if strict else (blk >= ch)
                        acc = acc + jnp.where(hit, 1.0, 0.0)
                    return acc

                u = COUNT_UNROLL
                acc = lax.fori_loop(0, n_sub // u, lambda c, a: scan(c * u * nrep, u * nrep, a),
                                    jnp.zeros((half, LANES), F32))
                acc = lax.fori_loop(u * (n_sub // u), n_sub, lambda c, a: scan(c * nrep, nrep, a), acc)
                accs.append(acc)
            total = jnp.sum(jnp.concatenate(accs, axis=0), axis=1, keepdims=True)
            return jnp.broadcast_to(total, (TQ, LANES))

        ga = gmax_ref[0]
        gb = gmax_ref[1]
        lo_f = jnp.min(jnp.minimum(ga, gb), axis=1, keepdims=True)
        hi_f = jnp.max(jnp.maximum(ga, gb), axis=1, keepdims=True)
        lo0 = jnp.broadcast_to(_float_to_key(lo_f), (TQ, LANES))
        hi0 = jnp.broadcast_to(_float_to_key(hi_f), (TQ, LANES))

        def unresolved(st):
            lo, hi = st
            return jnp.max(jnp.where(lo < hi, 1.0, 0.0)) > 0.5

        def narrow(lo, hi, mid):
            n_ge = count(_key_to_float(mid), False)
            ge = n_ge >= topk
            below = jnp.where(mid == MIN_NORMAL_KEY, 0, mid - 1)
            return (jnp.where(ge, mid, lo),
                    jnp.where(n_ge == topk, mid, jnp.where(ge, hi, below)))

        def bisect(st):
            lo, hi = st
            by_key = (lo >> 1) + (hi >> 1) + ((lo | hi) & 1)
            lo_f = _key_to_float(lo)
            hi_f = _key_to_float(hi)
            by_val = _float_to_key(lo_f + (hi_f - lo_f) * 0.5)
            by_val = jnp.minimum(jnp.maximum(by_val, lo + 1), hi)
            same_binade = (pltpu.bitcast(lo_f, jnp.int32) >> 23) == (pltpu.bitcast(hi_f, jnp.int32) >> 23)
            mid = jnp.where(same_binade, by_key, by_val)
            mid = jnp.where(lo < 0, jnp.where(hi >= 0, 0, mid), mid)
            mid = jnp.where(lo == 0, jnp.where(hi >= MIN_NORMAL_KEY, MIN_NORMAL_KEY, mid), mid)
            mid = jnp.where(lo < hi, mid, lo)
            return narrow(lo, hi, mid)

        state = narrow(lo0, hi0, hi0)
        state = lax.fori_loop(0, BLIND_PASSES, lambda _, st: bisect(st), state)
        lo, _ = lax.while_loop(unresolved, bisect, state)
        tau1 = _key_to_float(lo)
        keep1 = topk - count(tau1, True)
        tau = _tile_lanes(tau1, nrep)
        keep = _tile_lanes(keep1, nrep)

        def to_mask(c, masked):
            blk = s_load(c)
            eq = blk == tau
            rank = (_tile_lanes(cnt_ref[...], nrep)
                    + _dot(jnp.where(eq, 1.0, 0.0).astype(BF16), tri_ref[...]))
            keep_tie = jnp.where(rank <= keep, 0.0, NEG)
            bias = jnp.where(blk > tau, 0.0, jnp.where(eq, keep_tie, NEG))
            if masked:
                bias = jnp.where(admissible(c, bias.shape), bias, NEG)
            cnt_ref[...] = jnp.broadcast_to(rank[:, SUB - 1:SUB], (TQ, LANES))
            s_store(c, bias)

        def mask_body(c, carry):
            to_mask(c, False)
            return carry

        lax.fori_loop(0, last_sub, mask_body, 0)
        to_mask(last_sub, True)

        _flash_heads(q_ref, k_ref, v_ref, pl.ds(0, SUB), acc_ref, m_ref, masked_logits(0))

    @pl.when(j <= jmax)
    def _():
        def body(c, carry):
            rows = pl.ds(pl.multiple_of(c * SUB, SUB), SUB)
            _flash_heads_lazy(q_ref, k_ref, v_ref, rows, acc_ref, m_ref, tmp_ref,
                              masked_logits(j * nsub + c))
            return carry

        lax.fori_loop(jnp.where(j == 0, 1, 0), jnp.minimum(n_sub - j * nsub, nsub), body, 0)

        @pl.when(j == jmax)
        def _():
            _flash_finish(o_ref, acc_ref)


def _mixer_a(qi, wi, kie, kio, q, k, v):
    bsz, seq, _ = q.shape
    tkb, nkb, nsub = _kv_tiles(seq)
    topk = min(IDX_TOPK_MAX, seq // 4)
    tri = jnp.triu(jnp.ones((SUB, SUB), F32)).astype(BF16)
    qmap = lambda b, i, j: (b, i, 0)
    kv_idx = lambda b, i, j: (b, jnp.minimum(j, (i * TQ) // tkb), 0)
    res = lambda b, i, j: (b, 0, 0)
    return pl.pallas_call(
        functools.partial(_mixer_a_kernel, tkb=tkb, nsub=nsub, topk=topk),
        grid=(bsz, seq // TQ, nkb),
        in_specs=[pl.BlockSpec((None, TQ, IDX_HEADS * IDX_DIM), qmap),
                  pl.BlockSpec((None, TQ, LANES), qmap),
                  pl.BlockSpec((None, LANES, seq), res),
                  pl.BlockSpec((None, LANES, seq), res),
                  pl.BlockSpec((None, TQ, HW), qmap),
                  pl.BlockSpec((None, tkb, 2 * HW), kv_idx),
                  pl.BlockSpec((None, tkb, 2 * HW), kv_idx),
                  pl.BlockSpec((SUB, SUB), lambda b, i, j: (0, 0))],
        out_specs=pl.BlockSpec((None, TQ, HW), qmap),
        out_shape=jax.ShapeDtypeStruct((bsz, seq, HW), BF16),
        scratch_shapes=[pltpu.VMEM((seq // LANES, TQ, LANES), F32),
                        pltpu.VMEM((2, TQ, LANES), F32),
                        pltpu.VMEM((TQ, LANES), F32),
                        pltpu.VMEM((TQ, 2 * HW), F32),
                        pltpu.VMEM((N_HEADS, TQ, LANES), F32),
                        pltpu.VMEM((TQ, 2 * HW), F32)],
        compiler_params=_cparams(3),
        name="mixer_a",
    )(qi, wi, kie, kio, q, k, v, tri)


B_WIN = 3


def _window_heads(q_ref, o_ref, n_win, k_of, v_of, bias_of, sink_of=None):
    def scores(h):
        jp = h // 2
        qp = q_ref[:, jp * LANES:(jp + 1) * LANES]
        return [_dot_nt(qp, k_of(h, a)) + bias_of(h, a) for a in range(n_win)]

    s_next = scores(0)
    done = []
    for h in range(N_HEADS):
        s = s_next
        if h + 1 < N_HEADS:
            s_next = scores(h + 1)
        m = functools.reduce(jnp.maximum, [jnp.max(t, axis=1, keepdims=True) for t in s])
        if sink_of is not None:
            m = jnp.maximum(m, sink_of(h))
        pv = functools.reduce(
            jnp.add, [_dot(jnp.exp2(s[a] - m).astype(BF16), v_of(h, a)) for a in range(n_win)])
        lane = _denominator_lane(h)
        l = pv[:, lane:lane + 1]
        if sink_of is not None:
            l = l + jnp.exp2(sink_of(h) - m)
        done.append((pv, l))
        if h % 2 == 1:
            (pe, le), (po, lo_) = done[-2], done[-1]
            jp = h // 2
            o_ref[:, jp * LANES:(jp + 1) * LANES] = _pair_output(pe, po, le, lo_).astype(o_ref.dtype)


def _mixer_b_kernel(q_ref, k0_ref, k1_ref, k2_ref, v0_ref, v1_ref, v2_ref, bias_ref, o_ref):
    i = pl.program_id(1)
    k_refs = (k0_ref, k1_ref, k2_ref)
    v_refs = (v0_ref, v1_ref, v2_ref)
    pad = [jnp.where(i - (B_WIN - 1 - a) >= 0, 0.0, NEG) for a in range(B_WIN)]
    hc = lambda h: slice(h * LANES, (h + 1) * LANES)
    _window_heads(q_ref, o_ref, B_WIN,
                  lambda h, a: k_refs[a][:, hc(h)],
                  lambda h, a: v_refs[a][:, hc(h)],
                  lambda h, a: bias_ref[h, :, a * TQ:(a + 1) * TQ] + pad[a])


def _mixer_b(q, k, v, bias_tab):
    bsz, seq, _ = q.shape
    qmap = lambda b, i: (b, i, 0)
    kmaps = [functools.partial(lambda b, i, d: (b, jnp.maximum(i - d, 0), 0), d=B_WIN - 1 - a)
             for a in range(B_WIN)]
    kspec = [pl.BlockSpec((None, TQ, 2 * HW), km) for km in kmaps]
    return pl.pallas_call(
        _mixer_b_kernel,
        grid=(bsz, seq // TQ),
        in_specs=[pl.BlockSpec((None, TQ, HW), qmap)] + kspec + kspec
                 + [pl.BlockSpec(bias_tab.shape, lambda b, i: (0, 0, 0))],
        out_specs=pl.BlockSpec((None, TQ, HW), qmap),
        out_shape=jax.ShapeDtypeStruct((bsz, seq, HW), BF16),
        compiler_params=_cparams(2),
        name="mixer_b",
    )(q, k, k, k, v, v, v, bias_tab)


def _b_bias_table(rel_bias):
    win = B_WIN * TQ
    span = TQ + win
    u = np.arange(span)
    u = np.where(u >= win, u - span, u)
    idx = np.clip((win - TQ) - u, -B_REL_CLIP, B_REL_CLIP) + B_REL_CLIP
    v = rel_bias.astype(F32)[:, idx] * LOG2E
    tab = jnp.tile(v, (1, TQ))[:, :TQ * (span - 1)].reshape(N_HEADS, TQ, span - 1)[:, :, :win]
    t = np.arange(TQ)[:, None]
    jk = np.arange(win)[None, :]
    cdiff = (t // CHUNK + (B_WIN - 1) * TQ // CHUNK) - jk // CHUNK
    ok = (cdiff >= 0) & (cdiff <= B_PAST_CHUNKS)
    return jnp.where(jnp.asarray(ok)[None], tab, NEG)


def _mixer_d_kernel(q_ref, k0_ref, k1_ref, v0_ref, v1_ref, sink_ref, o_ref):
    i = pl.program_id(1)
    k_refs = (k0_ref, k1_ref)
    v_refs = (v0_ref, v1_ref)
    qpos = TQ + lax.broadcasted_iota(jnp.int32, (TQ, TQ), 0)
    bias = []
    for a in range(2):
        kpos = a * TQ + lax.broadcasted_iota(jnp.int32, (TQ, TQ), 1)
        cdiff = qpos // CHUNK - kpos // CHUNK
        ok = (cdiff >= 0) & (cdiff <= D_WINDOW_CHUNKS)
        if a == 0:
            ok = ok & (i > 0)
        bias.append(jnp.where(ok, 0.0, NEG))
    group = N_HEADS // D_KV_HEADS

    def piece(h):
        n = (h // group) * 2 + h % 2
        return slice(n * LANES, (n + 1) * LANES)

    _window_heads(q_ref, o_ref, 2,
                  lambda h, a: k_refs[a][:, piece(h)],
                  lambda h, a: v_refs[a][:, piece(h)],
                  lambda h, a: bias[a],
                  sink_of=lambda h: sink_ref[h:h + 1, 0:1])


def _mixer_d(q, k, v, sinks):
    bsz, seq, _ = q.shape
    qmap = lambda b, i: (b, i, 0)
    prev = lambda b, i: (b, jnp.maximum(i - 1, 0), 0)
    sink_tab = jnp.broadcast_to((sinks.astype(F32) * LOG2E)[:, None], (N_HEADS, LANES))
    return pl.pallas_call(
        _mixer_d_kernel,
        grid=(bsz, seq // TQ),
        in_specs=[pl.BlockSpec((None, TQ, HW), qmap),
                  pl.BlockSpec((None, TQ, HW), prev), pl.BlockSpec((None, TQ, HW), qmap),
                  pl.BlockSpec((None, TQ, HW), prev), pl.BlockSpec((None, TQ, HW), qmap),
                  pl.BlockSpec((N_HEADS, LANES), lambda b, i: (0, 0))],
        out_specs=pl.BlockSpec((None, TQ, HW), qmap),
        out_shape=jax.ShapeDtypeStruct((bsz, seq, HW), BF16),
        compiler_params=_cparams(2),
        name="mixer_d",
    )(q, k, k, v, v, sink_tab)


def _out_proj_kernel(h_ref, y0_ref, y1_ref, w0_ref, w1_ref, o_ref):
    o_ref[...] = h_ref[...] + _dot(y0_ref[...], w0_ref[...]) + _dot(y1_ref[...], w1_ref[...])


def _out_proj(h2, y0, y1, w0, w1, seq):
    m = h2.shape[0]
    tm = min(TM_PROJ, seq)
    row = lambda i: (i, 0)
    const = lambda i: (0, 0)
    return pl.pallas_call(
        _out_proj_kernel,
        grid=(m // tm,),
        in_specs=[pl.BlockSpec((tm, D_MODEL), row),
                  pl.BlockSpec((tm, HW), row), pl.BlockSpec((tm, HW), row),
                  pl.BlockSpec((HW, D_MODEL), const), pl.BlockSpec((HW, D_MODEL), const)],
        out_specs=pl.BlockSpec((tm, D_MODEL), row),
        out_shape=jax.ShapeDtypeStruct((m, D_MODEL), F32),
        compiler_params=_cparams(1),
        name="out_proj",
    )(h2, y0, y1, w0, w1)


def _ffn_kernel(h_ref, halo_ref, g_ref, wg_ref, wu_ref, cwg_ref, cwu_ref, cbg_ref, cbu_ref,
                wd_ref, go_ref, o_ref, xn_ref, xh_ref, acc_ref, hb_ref, *, nseq, final_norm):
    i = pl.program_id(0)
    f = pl.program_id(1)
    tm = h_ref.shape[0]

    @pl.when(f == 0)
    def _():
        xn_ref[...] = _rms(h_ref[...], g_ref[...]).astype(BF16)
        xh_ref[...] = _rms(halo_ref[...], g_ref[...]).astype(BF16)
        acc_ref[...] = jnp.zeros_like(acc_ref)

    halo_on = jnp.where(i % nseq == 0, 0.0, 1.0)

    def branch(w_ref, cw_ref, cb_ref):
        hb_ref[HALO:, :] = _dot(xn_ref[...], w_ref[...])
        hb_ref[:HALO, :] = _dot(xh_ref[...], w_ref[...]) * halo_on
        cw = cw_ref[...]
        return (hb_ref[pl.ds(HALO, tm), :] * cw[2:3, :]
                + hb_ref[pl.ds(HALO - 1, tm), :] * cw[1:2, :]
                + hb_ref[pl.ds(HALO - 2, tm), :] * cw[0:1, :]
                + cb_ref[...])

    yg = branch(wg_ref, cwg_ref, cbg_ref)
    yu = branch(wu_ref, cwu_ref, cbu_ref)
    act = (yg / (1.0 + jnp.exp(-yg))) * yu
    acc_ref[...] += _dot(act.astype(BF16), wd_ref[...])

    @pl.when(f == pl.num_programs(1) - 1)
    def _():
        out = h_ref[...] + acc_ref[...]
        if final_norm:
            out = _rms(out, go_ref[...])
        o_ref[...] = out


def _ffn(h2, g, w_up, conv_w, conv_b, w_down, g_out, seq, final_norm):
    m = h2.shape[0]
    tm = min(TM_FFN, seq)
    nseq = seq // tm
    nf = D_FF // TF
    row = lambda i, f: (i, 0)
    const = lambda i, f: (0, 0)
    halo = lambda i, f: (jnp.maximum(i * (tm // HALO) - 1, 0), 0)
    gcol = lambda i, f: (0, f)
    ucol = lambda i, f: (0, nf + f)
    return pl.pallas_call(
        functools.partial(_ffn_kernel, nseq=nseq, final_norm=final_norm),
        grid=(m // tm, nf),
        in_specs=[pl.BlockSpec((tm, D_MODEL), row),
                  pl.BlockSpec((HALO, D_MODEL), halo),
                  pl.BlockSpec((1, D_MODEL), const),
                  pl.BlockSpec((D_MODEL, TF), gcol), pl.BlockSpec((D_MODEL, TF), ucol),
                  pl.BlockSpec((3, TF), gcol), pl.BlockSpec((3, TF), ucol),
                  pl.BlockSpec((1, TF), gcol), pl.BlockSpec((1, TF), ucol),
                  pl.BlockSpec((TF, D_MODEL), lambda i, f: (f, 0)),
                  pl.BlockSpec((1, D_MODEL), const)],
        out_specs=pl.BlockSpec((tm, D_MODEL), row),
        out_shape=jax.ShapeDtypeStruct((m, D_MODEL), F32),
        scratch_shapes=[pltpu.VMEM((tm, D_MODEL), BF16),
                        pltpu.VMEM((HALO, D_MODEL), BF16),
                        pltpu.VMEM((tm, D_MODEL), F32),
                        pltpu.VMEM((tm + HALO, TF), F32)],
        compiler_params=_cparams(2),
        name="ffn",
    )(h2, h2, g, w_up, w_up, conv_w, conv_w, conv_b, conv_b, w_down, g_out)


def _rope_tables(seq):
    inv = 1.0 / (ROPE_THETA ** (jnp.arange(0, HEAD_DIM, 2, dtype=F32) / HEAD_DIM))
    ang = jnp.arange(seq, dtype=F32)[:, None] * inv[None, :]
    c, s = jnp.cos(ang), jnp.sin(ang)
    cos = jnp.concatenate([c, c, c, c], axis=1)
    sin = jnp.concatenate([-s, s, -s, s], axis=1)
    return cos, sin


def _pad_cols(w, width):
    return jnp.pad(w, ((0, 0), (0, width - w.shape[1])))


def _even_weights(w_in):
    o = np.cumsum([0, 512, 512, 512, IDX_HEADS * IDX_DIM, IDX_DIM, IDX_HEADS, 512, 512, 512])
    sl = [w_in[:, o[n]:o[n + 1]] for n in range(9)]
    small = jnp.concatenate([_pad_cols(sl[4], LANES), _pad_cols(sl[5], LANES)], axis=1)
    return jnp.concatenate([sl[0], sl[1], sl[2], sl[3], small, sl[6], sl[7], sl[8]],
                           axis=1).astype(BF16)


def _odd_weights(w_in):
    o = np.cumsum([0, 512, 512, 512, N_HEADS, 512, D_KV_HEADS * HEAD_DIM, D_KV_HEADS * HEAD_DIM])
    sl = [w_in[:, o[n]:o[n + 1]] for n in range(7)]
    return jnp.concatenate([sl[0], sl[1], sl[2], _pad_cols(sl[3], LANES), sl[4], sl[5], sl[6]],
                           axis=1).astype(BF16)


def _row(v, width=None):
    v = v.astype(F32)[None, :]
    return v if width is None else _pad_cols(v, width)


def kernel(x, norm_mix_g, norm_ffn_g, norm_out_g, even_w_in, even_w_out, idx_k_ln_g, idx_k_ln_b,
           rel_bias, odd_w_in, odd_w_out, forget_b, sinks, ffn_w_up, ffn_conv_w, ffn_conv_b,
           ffn_w_down):
    bsz, seq, _ = x.shape
    m = bsz * seq
    depth = norm_mix_g.shape[0]
    assert seq % TQ == 0 and seq % min(TM_PROJ, seq) == 0 and seq % min(TKB_MAX, seq) == 0
    cos, sin = _rope_tables(seq)
    h = x.reshape(m, D_MODEL)
    as3 = lambda t: t.reshape(bsz, seq, t.shape[-1])

    for layer in range(depth):
        jj = layer // 2
        g_mix = _row(norm_mix_g[layer])
        if layer % 2 == 0:
            qa, ka, va, qi, ki, wi, qb, kb, vb = _even_proj(
                h, g_mix, _even_weights(even_w_in[jj]), cos, sin,
                _row(idx_k_ln_g[jj], LANES), _row(idx_k_ln_b[jj], LANES), seq)
            kit = jnp.swapaxes(as3(ki), 1, 2)
            kie, kio = kit, jnp.roll(kit, IDX_DIM, axis=1)
            y0 = _mixer_a(as3(qi), as3(wi), kie, kio, as3(qa), as3(ka), as3(va))
            y1 = _mixer_b(as3(qb), as3(kb), as3(vb), _b_bias_table(rel_bias[jj]))
            w_out = even_w_out[jj]
        else:
            qc, kc, vc, fcum, qd, kd, vd = _odd_proj(
                h, g_mix, _odd_weights(odd_w_in[jj]), cos, sin,
                _row(forget_b[jj], LANES), seq)
            fq = as3(fcum)
            fk = jnp.swapaxes(fq[:, :, :N_HEADS], 1, 2)
            y0 = _mixer_c(as3(qc), as3(kc), as3(vc), fq, fk)
            y1 = _mixer_d(as3(qd), as3(kd), as3(vd), sinks[jj])
            w_out = odd_w_out[jj]
        w_out = w_out.astype(BF16)
        h = _out_proj(h, y0.reshape(m, HW), y1.reshape(m, HW), w_out[:HW], w_out[HW:], seq)
        h = _ffn(h, _row(norm_ffn_g[layer]), ffn_w_up[layer].astype(BF16),
                 ffn_conv_w[layer].astype(F32), _row(ffn_conv_b[layer]),
                 ffn_w_down[layer].astype(BF16), _row(norm_out_g), seq,
                 final_norm=(layer == depth - 1))
    return h.reshape(bsz, seq, D_MODEL)
```

```python
import functools

import numpy as np
import jax
import jax.numpy as jnp
from jax import lax
from jax.experimental import pallas as pl
from jax.experimental.pallas import tpu as pltpu

F32 = jnp.float32
BF16 = jnp.bfloat16

D_MODEL = 1024
HEAD_DIM = 64
CHUNK = 64
ROPE_THETA = 10000.0
EPS = 1e-6
NEG = -1e30
N_HEADS = 8
IDX_HEADS = 4
IDX_DIM = 64
IDX_TOPK_MAX = 256
B_PAST_CHUNKS = 8
B_REL_CLIP = 256
D_KV_HEADS = 2
D_WINDOW_CHUNKS = 2
D_FF = 2816
HW = N_HEADS * HEAD_DIM
SCALE = HEAD_DIM ** -0.5
LOG2E = 1.4426950408889634
QSCALE = SCALE * LOG2E

LANES = 128
VMEM_LIMIT = 56 * 1024 * 1024

TM_PROJ = 512
TM_FFN = 1024
TQ = 256
TQ_C = 256
SUB = 512
TKB_MAX = 2048
TF = 256
HALO = 16
LAZY_MARGIN = 20.0


def _cparams(n_axes):
    return pltpu.CompilerParams(
        dimension_semantics=("arbitrary",) * n_axes,
        vmem_limit_bytes=VMEM_LIMIT)


def _rms(x, g):
    ms = jnp.mean(x * x, axis=-1, keepdims=True)
    return x * lax.rsqrt(ms + EPS) * g


def _lane_iota(shape):
    return lax.broadcasted_iota(jnp.int32, shape, len(shape) - 1)


def _rope(t, cos, sin):
    w = t.shape[1]
    lane = _lane_iota(t.shape)
    fwd = pltpu.roll(t, 32, 1)
    bwd = pltpu.roll(t, w - 32, 1)
    rot = jnp.where((lane % HEAD_DIM) < (HEAD_DIM // 2), bwd, fwd)
    return t * cos + rot * sin


def _tile_lanes(t, reps):
    return t if reps == 1 else jnp.concatenate([t] * reps, axis=1)


def _head_pieces(p, sw=None):
    lane = _lane_iota(p.shape)
    lo = lane < HEAD_DIM
    q = p if sw is None else sw
    even = jnp.where(lane == HEAD_DIM, 1.0, jnp.where(lo, p, 0.0))
    odd = jnp.where(lane == 0, 1.0, jnp.where(lo, 0.0, q))
    return even, odd


def _expand_heads(t, ones_lane=False):
    pieces = []
    for j in range(t.shape[1] // LANES):
        p = t[:, j * LANES:(j + 1) * LANES]
        if ones_lane:
            pieces.extend(_head_pieces(p))
        else:
            lo = _lane_iota(p.shape) < HEAD_DIM
            pieces.append(jnp.where(lo, p, 0.0))
            pieces.append(jnp.where(lo, 0.0, p))
    return jnp.concatenate(pieces, axis=1).astype(BF16)


def _expand_kv_pair(t, ones_lane=False):
    lo = _lane_iota(t.shape) < HEAD_DIM
    sw = pltpu.roll(t, HEAD_DIM, 1)
    if ones_lane:
        e0, o0 = _head_pieces(t, sw)
        e1, o1 = _head_pieces(sw, t)
        pieces = [e0, o0, e1, o1]
    else:
        pieces = [jnp.where(lo, t, 0.0), jnp.where(lo, 0.0, sw),
                  jnp.where(lo, sw, 0.0), jnp.where(lo, 0.0, t)]
    return jnp.concatenate(pieces, axis=1).astype(BF16)


def _dot(a, b):
    return jnp.dot(a, b, preferred_element_type=F32)


def _dot_nt(a, b):
    return lax.dot_general(a, b, (((1,), (1,)), ((), ())), preferred_element_type=F32)


def _even_proj_kernel(x_ref, g_ref, w_ref, cos_ref, sin_ref, lng_ref, lnb_ref,
                      qa_ref, ka_ref, va_ref, qi_ref, ki_ref, wi_ref,
                      qb_ref, kb_ref, vb_ref):
    xn = _rms(x_ref[...], g_ref[...]).astype(BF16)
    cos = cos_ref[...]
    sin = sin_ref[...]
    cos4 = _tile_lanes(cos, 4)
    sin4 = _tile_lanes(sin, 4)

    def proj(a, b):
        return _dot(xn, w_ref[:, a:b])

    qa_ref[...] = (_rope(proj(0, 512), cos4, sin4) * QSCALE).astype(BF16)
    ka_ref[...] = _expand_heads(_rope(proj(512, 1024), cos4, sin4))
    va_ref[...] = _expand_heads(proj(1024, 1536), ones_lane=True)
    qi_ref[...] = _rope(proj(1536, 1792), _tile_lanes(cos, 2), _tile_lanes(sin, 2)).astype(BF16)
    small = proj(1792, 2048)
    kraw = small[:, :LANES]
    wi_ref[...] = small[:, LANES:]
    valid = _lane_iota(kraw.shape) < IDX_DIM
    mean = jnp.sum(kraw, axis=-1, keepdims=True) * (1.0 / IDX_DIM)
    xc = jnp.where(valid, kraw - mean, 0.0)
    var = jnp.sum(xc * xc, axis=-1, keepdims=True) * (1.0 / IDX_DIM)
    kn = xc * lax.rsqrt(var + EPS) * lng_ref[...] + lnb_ref[...]
    ki_ref[...] = _rope(kn, cos, sin).astype(BF16)
    qb_ref[...] = (proj(2048, 2560) * QSCALE).astype(BF16)
    kb_ref[...] = _expand_heads(proj(2560, 3072))
    vb_ref[...] = _expand_heads(proj(3072, 3584), ones_lane=True)


def _even_proj(x2, g, w, cos, sin, lng, lnb, seq):
    m = x2.shape[0]
    tm = min(TM_PROJ, seq)
    nseq = seq // tm
    row = lambda i: (i, 0)
    const = lambda i: (0, 0)
    tab = lambda i: (i % nseq, 0)
    widths = [(512, BF16), (1024, BF16), (1024, BF16), (256, BF16), (128, BF16),
              (128, F32), (512, BF16), (1024, BF16), (1024, BF16)]
    return pl.pallas_call(
        _even_proj_kernel,
        grid=(m // tm,),
        in_specs=[pl.BlockSpec((tm, D_MODEL), row),
                  pl.BlockSpec((1, D_MODEL), const),
                  pl.BlockSpec(w.shape, const),
                  pl.BlockSpec((tm, LANES), tab),
                  pl.BlockSpec((tm, LANES), tab),
                  pl.BlockSpec((1, LANES), const),
                  pl.BlockSpec((1, LANES), const)],
        out_specs=[pl.BlockSpec((tm, wd), row) for wd, _ in widths],
        out_shape=[jax.ShapeDtypeStruct((m, wd), dt) for wd, dt in widths],
        compiler_params=_cparams(1),
        name="even_proj",
    )(x2, g, w, cos, sin, lng, lnb)


def _odd_proj_kernel(x_ref, g_ref, w_ref, cos_ref, sin_ref, fb_ref, tril_ref,
                     qc_ref, kc_ref, vc_ref, f_ref, qd_ref, kd_ref, vd_ref,
                     carry_ref, *, nseq):
    i = pl.program_id(0)
    xn = _rms(x_ref[...], g_ref[...]).astype(BF16)
    cos = cos_ref[...]
    sin = sin_ref[...]

    def proj(a, b):
        return _dot(xn, w_ref[:, a:b])

    qc_ref[...] = (proj(0, 512) * QSCALE).astype(BF16)
    kc_ref[...] = _expand_heads(proj(512, 1024))
    vc_ref[...] = _expand_heads(proj(1024, 1536), ones_lane=True)

    z = proj(1536, 1664) + fb_ref[...]
    lf = jnp.minimum(z, 0.0) - jnp.log(1.0 + jnp.exp(-jnp.abs(z)))
    lf = jnp.where(_lane_iota(lf.shape) < N_HEADS, lf, 0.0)
    hi = lf.astype(BF16)
    r1 = lf - hi.astype(F32)
    mid = r1.astype(BF16)
    lo = (r1 - mid.astype(F32)).astype(BF16)
    tril = tril_ref[...]
    csum = _dot(tril, hi) + _dot(tril, mid) + _dot(tril, lo)

    @pl.when(i % nseq == 0)
    def _():
        carry_ref[...] = jnp.zeros_like(carry_ref)

    fcum = csum + carry_ref[0:1, :]
    f_ref[...] = fcum * LOG2E
    carry_ref[...] = jnp.broadcast_to(fcum[fcum.shape[0] - 1:, :], carry_ref.shape)

    qd_ref[...] = (_rope(proj(1664, 2176), _tile_lanes(cos, 4), _tile_lanes(sin, 4))
                   * QSCALE).astype(BF16)
    kv = proj(2176, 2432)
    kd_ref[...] = _expand_kv_pair(_rope(kv[:, :LANES], cos, sin))
    vd_ref[...] = _expand_kv_pair(kv[:, LANES:], ones_lane=True)


def _odd_proj(x2, g, w, cos, sin, fb, seq):
    m = x2.shape[0]
    tm = min(TM_PROJ, seq)
    nseq = seq // tm
    row = lambda i: (i, 0)
    const = lambda i: (0, 0)
    tab = lambda i: (i % nseq, 0)
    tril = jnp.tril(jnp.ones((tm, tm), F32)).astype(BF16)
    widths = [(512, BF16), (1024, BF16), (1024, BF16), (128, F32),
              (512, BF16), (512, BF16), (512, BF16)]
    return pl.pallas_call(
        functools.partial(_odd_proj_kernel, nseq=nseq),
        grid=(m // tm,),
        in_specs=[pl.BlockSpec((tm, D_MODEL), row),
                  pl.BlockSpec((1, D_MODEL), const),
                  pl.BlockSpec(w.shape, const),
                  pl.BlockSpec((tm, LANES), tab),
                  pl.BlockSpec((tm, LANES), tab),
                  pl.BlockSpec((1, LANES), const),
                  pl.BlockSpec((tm, tm), const)],
        out_specs=[pl.BlockSpec((tm, wd), row) for wd, _ in widths],
        out_shape=[jax.ShapeDtypeStruct((m, wd), dt) for wd, dt in widths],
        scratch_shapes=[pltpu.VMEM((8, LANES), F32)],
        compiler_params=_cparams(1),
        name="odd_proj",
    )(x2, g, w, cos, sin, fb, tril)


def _denominator_lane(h):
    return HEAD_DIM if h % 2 == 0 else 0


def _flash_heads(q_ref, k_ref, v_ref, rows, acc_ref, m_ref, logit_fn):
    nrep = SUB // LANES
    s_next = _head_scores(q_ref, k_ref, rows, 0)
    for h in range(N_HEADS):
        s_raw = s_next
        if h + 1 < N_HEADS:
            s_next = _head_scores(q_ref, k_ref, rows, h + 1)
        s, shift = logit_fn(h, s_raw)
        m_prev = m_ref[h]
        m_cur = jnp.max(s, axis=1, keepdims=True)
        if shift is not None:
            m_cur = m_cur + shift
        m_new = jnp.maximum(m_prev, m_cur)
        alpha = jnp.exp2(m_prev - m_new)
        off = m_new if shift is None else m_new - shift
        p = jnp.exp2(s - _tile_lanes(off, nrep))
        m_ref[h] = m_new
        cols = slice(h * LANES, (h + 1) * LANES)
        acc_ref[:, cols] = acc_ref[:, cols] * alpha + _dot(p.astype(BF16), v_ref[rows, cols])


def _head_scores(q_ref, k_ref, rows, h):
    jp = h // 2
    return _dot_nt(q_ref[:, jp * LANES:(jp + 1) * LANES], k_ref[rows, h * LANES:(h + 1) * LANES])


def _flash_heads_lazy(q_ref, k_ref, v_ref, rows, acc_ref, m_ref, tmp_ref, logit_fn):
    nrep = SUB // LANES
    worst = None
    pending = [_head_scores(q_ref, k_ref, rows, h) for h in range(2)]
    for h in range(N_HEADS):
        if h + 2 < N_HEADS:
            pending.append(_head_scores(q_ref, k_ref, rows, h + 2))
        s, shift = logit_fn(h, pending.pop(0))
        ref = m_ref[h] if shift is None else m_ref[h] - shift
        lane_max = functools.reduce(
            jnp.maximum, [s[:, t * LANES:(t + 1) * LANES] for t in range(nrep)])
        over = lane_max - ref
        worst = over if worst is None else jnp.maximum(worst, over)
        p = jnp.exp2(s - _tile_lanes(ref, nrep))
        cols = slice(h * LANES, (h + 1) * LANES)
        tmp_ref[:, cols] = _dot(p.astype(BF16), v_ref[rows, cols])
    exceeded = jnp.max(worst) > LAZY_MARGIN

    @pl.when(jnp.logical_not(exceeded))
    def _():
        acc_ref[...] += tmp_ref[...]

    @pl.when(exceeded)
    def _():
        _flash_heads(q_ref, k_ref, v_ref, rows, acc_ref, m_ref, logit_fn)


def _flash_init(acc_ref, m_ref):
    acc_ref[...] = jnp.zeros_like(acc_ref)
    m_ref[...] = jnp.full(m_ref.shape, NEG, F32)


def _pair_output(even, odd, l_even, l_odd):
    lo = _lane_iota(even.shape) < HEAD_DIM
    return jnp.where(lo, even / l_even, odd / l_odd)


def _flash_finish(o_ref, acc_ref):
    for jp in range(N_HEADS // 2):
        even = acc_ref[:, (2 * jp) * LANES:(2 * jp + 1) * LANES]
        odd = acc_ref[:, (2 * jp + 1) * LANES:(2 * jp + 2) * LANES]
        le = even[:, _denominator_lane(0):_denominator_lane(0) + 1]
        lo_ = odd[:, _denominator_lane(1):_denominator_lane(1) + 1]
        o_ref[:, jp * LANES:(jp + 1) * LANES] = _pair_output(even, odd, le, lo_).astype(o_ref.dtype)


def _kv_tiles(seq):
    tkb = min(TKB_MAX, seq)
    return tkb, seq // tkb, tkb // SUB


def _mixer_c_kernel(q_ref, k_ref, v_ref, fq_ref, fk_ref, o_ref, acc_ref, m_ref, fqr_ref,
                    *, tq, tkb, nsub):
    i = pl.program_id(1)
    j = pl.program_id(2)
    q_lo = i * tq
    last_sub = (q_lo + tq - 1) // SUB
    first_masked = q_lo // SUB
    jmax = last_sub // nsub

    @pl.when(j == 0)
    def _():
        _flash_init(acc_ref, m_ref)
        for h in range(N_HEADS):
            fqr_ref[h] = jnp.broadcast_to(fq_ref[:, h:h + 1], (tq, LANES))

    def step(c, masked):
        rows = pl.ds(pl.multiple_of(c * SUB, SUB), SUB)

        def logit_fn(h, s):
            s = s - fk_ref[h:h + 1, rows]
            if masked:
                kpos = j * tkb + c * SUB + lax.broadcasted_iota(jnp.int32, s.shape, 1)
                qpos = q_lo + lax.broadcasted_iota(jnp.int32, s.shape, 0)
                s = jnp.where(kpos <= qpos, s, NEG)
            return s, fqr_ref[h]

        _flash_heads(q_ref, k_ref, v_ref, rows, acc_ref, m_ref, logit_fn)

    @pl.when(j <= jmax)
    def _():
        lo = j * nsub
        n_full = jnp.clip(first_masked - lo, 0, nsub)
        n_all = jnp.clip(last_sub + 1 - lo, 0, nsub)

        def body(c, carry):
            step(c, False)
            return carry

        def masked_body(c, carry):
            step(c, True)
            return carry

        lax.fori_loop(0, n_full, body, 0)
        lax.fori_loop(n_full, n_all, masked_body, 0)

        @pl.when(j == jmax)
        def _():
            _flash_finish(o_ref, acc_ref)


def _mixer_c(q, k, v, fq, fk):
    bsz, seq, _ = q.shape
    tq = min(TQ_C, seq)
    tkb, nkb, nsub = _kv_tiles(seq)
    kv_idx = lambda b, i, j: (b, jnp.minimum(j, (i * tq + tq - 1) // tkb), 0)
    return pl.pallas_call(
        functools.partial(_mixer_c_kernel, tq=tq, tkb=tkb, nsub=nsub),
        grid=(bsz, seq // tq, nkb),
        in_specs=[pl.BlockSpec((None, tq, HW), lambda b, i, j: (b, i, 0)),
                  pl.BlockSpec((None, tkb, 2 * HW), kv_idx),
                  pl.BlockSpec((None, tkb, 2 * HW), kv_idx),
                  pl.BlockSpec((None, tq, LANES), lambda b, i, j: (b, i, 0)),
                  pl.BlockSpec((None, N_HEADS, tkb),
                               lambda b, i, j: (b, 0, jnp.minimum(j, (i * tq + tq - 1) // tkb)))],
        out_specs=pl.BlockSpec((None, tq, HW), lambda b, i, j: (b, i, 0)),
        out_shape=jax.ShapeDtypeStruct((bsz, seq, HW), BF16),
        scratch_shapes=[pltpu.VMEM((tq, 2 * HW), F32),
                        pltpu.VMEM((N_HEADS, tq, LANES), F32),
                        pltpu.VMEM((N_HEADS, tq, LANES), F32)],
        compiler_params=_cparams(3),
        name="mixer_c",
    )(q, k, v, fq, fk)


MIN_NORMAL_KEY = 0x00800000
BLIND_PASSES = 10
COUNT_UNROLL = 4


def _key_flip(bits):
    return jnp.where(bits >= 0, bits, bits ^ jnp.int32(0x7FFFFFFF))


def _float_to_key(f):
    return _key_flip(pltpu.bitcast(f, jnp.int32))


def _key_to_float(k):
    return pltpu.bitcast(_key_flip(k), F32)


def _mixer_a_kernel(qi_ref, wi_ref, kie_ref, kio_ref, q_ref, k_ref, v_ref, tri_ref,
                    o_ref, s_ref, gmax_ref, cnt_ref, acc_ref, m_ref, tmp_ref,
                    *, tkb, nsub, topk):
    i = pl.program_id(1)
    j = pl.program_id(2)
    q_lo = i * TQ
    last_sub = q_lo // SUB
    jmax = last_sub // nsub
    n_sub = last_sub + 1
    nrep = SUB // LANES
    half = TQ // 2

    def admissible(c_glob, shape):
        kpos = c_glob * SUB + lax.broadcasted_iota(jnp.int32, shape, 1)
        qpos = q_lo + lax.broadcasted_iota(jnp.int32, shape, 0)
        return (kpos // CHUNK) <= (qpos // CHUNK)

    def s_load(c):
        return jnp.concatenate([s_ref[c * nrep + t] for t in range(nrep)], axis=1)

    def s_store(c, val):
        for t in range(nrep):
            s_ref[c * nrep + t] = val[:, t * LANES:(t + 1) * LANES]

    def masked_logits(c_glob):
        return lambda h, s: (s + s_load(c_glob), None)

    @pl.when(j == 0)
    def _():
        _flash_init(acc_ref, m_ref)
        cnt_ref[...] = jnp.zeros_like(cnt_ref)
        gmax_ref[...] = jnp.full(gmax_ref.shape, -jnp.inf, F32)

        def score(c, masked):
            cols = pl.ds(pl.multiple_of(c * SUB, SUB), SUB)
            tot = None
            for pair in range(IDX_HEADS // 2):
                qp = qi_ref[:, pair * LANES:(pair + 1) * LANES]
                for e in range(2):
                    h = 2 * pair + e
                    kt = (kie_ref if e == 0 else kio_ref)[:, cols]
                    term = jnp.maximum(_dot(qp, kt), 0.0) * wi_ref[:, h:h + 1]
                    tot = term if tot is None else tot + term
            if masked:
                tot = jnp.where(admissible(c, tot.shape), tot, NEG)
            s_store(c, tot)
            for par in range(2):
                g = gmax_ref[par]
                for t in range(par, nrep, 2):
                    g = jnp.maximum(g, tot[:, t * LANES:(t + 1) * LANES])
                gmax_ref[par] = g

        def score_body(c, carry):
            score(c, False)
            return carry

        lax.fori_loop(0, last_sub, score_body, 0)
        score(last_sub, True)

        def count(cand):
            accs = []
            for hh in range(2):
                r0 = hh * half
                ch = cand[r0:r0 + half]

                def scan(t0, width, acc):
                    for t in range(width):
                        blk = s_ref[t0 + t, r0:r0 + half, :]
                        acc = acc + jnp.where(blk >= ch, 1.0, 0.0)
                    return acc

                u = COUNT_UNROLL
                acc = lax.fori_loop(0, n_sub // u, lambda c, a: scan(c * u * nrep, u * nrep, a),
                                    jnp.zeros((half, LANES), F32))
                acc = lax.fori_loop(u * (n_sub // u), n_sub, lambda c, a: scan(c * nrep, nrep, a), acc)
                accs.append(acc)
            total = jnp.sum(jnp.concatenate(accs, axis=0), axis=1, keepdims=True)
            return jnp.broadcast_to(total, (TQ, LANES))

        ga = gmax_ref[0]
        gb = gmax_ref[1]
        lo_f = jnp.min(jnp.minimum(ga, gb), axis=1, keepdims=True)
        hi_f = jnp.max(jnp.maximum(ga, gb), axis=1, keepdims=True)
        lo0 = jnp.broadcast_to(_float_to_key(lo_f), (TQ, LANES))
        hi0 = jnp.broadcast_to(_float_to_key(hi_f), (TQ, LANES))

        def unresolved(st):
            lo, hi, _ = st
            return jnp.max(jnp.where(lo < hi, 1.0, 0.0)) > 0.5

        def narrow(st, mid):
            lo, hi, n_above = st
            n_ge = count(_key_to_float(mid))
            ge = n_ge >= topk
            exact = n_ge == topk
            below = jnp.where(mid == MIN_NORMAL_KEY, 0, mid - 1)
            return (jnp.where(ge, mid, lo),
                    jnp.where(exact, mid, jnp.where(ge, hi, below)),
                    jnp.where(exact, 0.0, jnp.where(ge, n_above, n_ge)))

        def bisect(st):
            lo, hi, _ = st
            by_key = (lo >> 1) + (hi >> 1) + ((lo | hi) & 1)
            lo_f = _key_to_float(lo)
            hi_f = _key_to_float(hi)
            by_val = _float_to_key(lo_f + (hi_f - lo_f) * 0.5)
            by_val = jnp.minimum(jnp.maximum(by_val, lo + 1), hi)
            same_binade = (pltpu.bitcast(lo_f, jnp.int32) >> 23) == (pltpu.bitcast(hi_f, jnp.int32) >> 23)
            mid = jnp.where(same_binade, by_key, by_val)
            mid = jnp.where(lo < 0, jnp.where(hi >= 0, 0, mid), mid)
            mid = jnp.where(lo == 0, jnp.where(hi >= MIN_NORMAL_KEY, MIN_NORMAL_KEY, mid), mid)
            mid = jnp.where(lo < hi, mid, lo)
            return narrow(st, mid)

        state = narrow((lo0, hi0, jnp.zeros((TQ, LANES), F32)), hi0)
        state = lax.fori_loop(0, BLIND_PASSES, lambda _, st: bisect(st), state)
        lo, _, n_above = lax.while_loop(unresolved, bisect, state)
        tau = _tile_lanes(_key_to_float(lo), nrep)
        keep = _tile_lanes(topk - n_above, nrep)

        def to_mask(c, masked):
            blk = s_load(c)
            eq = blk == tau
            rank = (_tile_lanes(cnt_ref[...], nrep)
                    + _dot(jnp.where(eq, 1.0, 0.0).astype(BF16), tri_ref[...]))
            keep_tie = jnp.where(rank <= keep, 0.0, NEG)
            bias = jnp.where(blk > tau, 0.0, jnp.where(eq, keep_tie, NEG))
            if masked:
                bias = jnp.where(admissible(c, bias.shape), bias, NEG)
            cnt_ref[...] = jnp.broadcast_to(rank[:, SUB - 1:SUB], (TQ, LANES))
            s_store(c, bias)

        def mask_body(c, carry):
            to_mask(c, False)
            return carry

        lax.fori_loop(0, last_sub, mask_body, 0)
        to_mask(last_sub, True)

        _flash_heads(q_ref, k_ref, v_ref, pl.ds(0, SUB), acc_ref, m_ref, masked_logits(0))

    @pl.when(j <= jmax)
    def _():
        def body(c, carry):
            rows = pl.ds(pl.multiple_of(c * SUB, SUB), SUB)
            _flash_heads_lazy(q_ref, k_ref, v_ref, rows, acc_ref, m_ref, tmp_ref,
                              masked_logits(j * nsub + c))
            return carry

        lax.fori_loop(jnp.where(j == 0, 1, 0), jnp.minimum(n_sub - j * nsub, nsub), body, 0)

        @pl.when(j == jmax)
        def _():
            _flash_finish(o_ref, acc_ref)


def _mixer_a(qi, wi, kie, kio, q, k, v):
    bsz, seq, _ = q.shape
    tkb, nkb, nsub = _kv_tiles(seq)
    topk = min(IDX_TOPK_MAX, seq // 4)
    tri = jnp.triu(jnp.ones((SUB, SUB), F32)).astype(BF16)
    qmap = lambda b, i, j: (b, i, 0)
    kv_idx = lambda b, i, j: (b, jnp.minimum(j, (i * TQ) // tkb), 0)
    res = lambda b, i, j: (b, 0, 0)
    return pl.pallas_call(
        functools.partial(_mixer_a_kernel, tkb=tkb, nsub=nsub, topk=topk),
        grid=(bsz, seq // TQ, nkb),
        in_specs=[pl.BlockSpec((None, TQ, IDX_HEADS * IDX_DIM), qmap),
                  pl.BlockSpec((None, TQ, LANES), qmap),
                  pl.BlockSpec((None, LANES, seq), res),
                  pl.BlockSpec((None, LANES, seq), res),
                  pl.BlockSpec((None, TQ, HW), qmap),
                  pl.BlockSpec((None, tkb, 2 * HW), kv_idx),
                  pl.BlockSpec((None, tkb, 2 * HW), kv_idx),
                  pl.BlockSpec((SUB, SUB), lambda b, i, j: (0, 0))],
        out_specs=pl.BlockSpec((None, TQ, HW), qmap),
        out_shape=jax.ShapeDtypeStruct((bsz, seq, HW), BF16),
        scratch_shapes=[pltpu.VMEM((seq // LANES, TQ, LANES), F32),
                        pltpu.VMEM((2, TQ, LANES), F32),
                        pltpu.VMEM((TQ, LANES), F32),
                        pltpu.VMEM((TQ, 2 * HW), F32),
                        pltpu.VMEM((N_HEADS, TQ, LANES), F32),
                        pltpu.VMEM((TQ, 2 * HW), F32)],
        compiler_params=_cparams(3),
        name="mixer_a",
    )(qi, wi, kie, kio, q, k, v, tri)


B_WIN = 3


def _window_heads(q_ref, o_ref, n_win, k_of, v_of, bias_of, sink_of=None):
    def scores(h):
        jp = h // 2
        qp = q_ref[:, jp * LANES:(jp + 1) * LANES]
        return [_dot_nt(qp, k_of(h, a)) + bias_of(h, a) for a in range(n_win)]

    s_next = scores(0)
    done = []
    for h in range(N_HEADS):
        s = s_next
        if h + 1 < N_HEADS:
            s_next = scores(h + 1)
        lane_max = functools.reduce(
            jnp.maximum, [t[:, n * LANES:(n + 1) * LANES] for t in s for n in range(t.shape[1] // LANES)])
        m = jnp.max(lane_max, axis=1, keepdims=True)
        if sink_of is not None:
            m = jnp.maximum(m, sink_of(h))
        m_rep = jnp.broadcast_to(m, lane_max.shape)
        pv = functools.reduce(
            jnp.add, [_dot(jnp.exp2(s[a] - _tile_lanes(m_rep, s[a].shape[1] // LANES)).astype(BF16),
                           v_of(h, a)) for a in range(n_win)])
        lane = _denominator_lane(h)
        l = pv[:, lane:lane + 1]
        if sink_of is not None:
            l = l + jnp.exp2(sink_of(h) - m)
        done.append((pv, l))
        if h % 2 == 1:
            (pe, le), (po, lo_) = done[-2], done[-1]
            jp = h // 2
            o_ref[:, jp * LANES:(jp + 1) * LANES] = _pair_output(pe, po, le, lo_).astype(o_ref.dtype)


def _mixer_b_kernel(q_ref, k0_ref, k1_ref, k2_ref, v0_ref, v1_ref, v2_ref, bias_ref, o_ref):
    i = pl.program_id(1)
    k_refs = (k0_ref, k1_ref, k2_ref)
    v_refs = (v0_ref, v1_ref, v2_ref)
    pad = [jnp.where(i - (B_WIN - 1 - a) >= 0, 0.0, NEG) for a in range(B_WIN)]
    hc = lambda h: slice(h * LANES, (h + 1) * LANES)
    _window_heads(q_ref, o_ref, B_WIN,
                  lambda h, a: k_refs[a][:, hc(h)],
                  lambda h, a: v_refs[a][:, hc(h)],
                  lambda h, a: bias_ref[h, :, a * TQ:(a + 1) * TQ] + pad[a])


def _mixer_b(q, k, v, bias_tab):
    bsz, seq, _ = q.shape
    qmap = lambda b, i: (b, i, 0)
    kmaps = [functools.partial(lambda b, i, d: (b, jnp.maximum(i - d, 0), 0), d=B_WIN - 1 - a)
             for a in range(B_WIN)]
    kspec = [pl.BlockSpec((None, TQ, 2 * HW), km) for km in kmaps]
    return pl.pallas_call(
        _mixer_b_kernel,
        grid=(bsz, seq // TQ),
        in_specs=[pl.BlockSpec((None, TQ, HW), qmap)] + kspec + kspec
                 + [pl.BlockSpec(bias_tab.shape, lambda b, i: (0, 0, 0))],
        out_specs=pl.BlockSpec((None, TQ, HW), qmap),
        out_shape=jax.ShapeDtypeStruct((bsz, seq, HW), BF16),
        compiler_params=_cparams(2),
        name="mixer_b",
    )(q, k, k, k, v, v, v, bias_tab)


def _b_bias_table(rel_bias):
    win = B_WIN * TQ
    span = TQ + win
    u = np.arange(span)
    u = np.where(u >= win, u - span, u)
    idx = np.clip((win - TQ) - u, -B_REL_CLIP, B_REL_CLIP) + B_REL_CLIP
    v = rel_bias.astype(F32)[:, idx] * LOG2E
    tab = jnp.tile(v, (1, TQ))[:, :TQ * (span - 1)].reshape(N_HEADS, TQ, span - 1)[:, :, :win]
    t = np.arange(TQ)[:, None]
    jk = np.arange(win)[None, :]
    cdiff = (t // CHUNK + (B_WIN - 1) * TQ // CHUNK) - jk // CHUNK
    ok = (cdiff >= 0) & (cdiff <= B_PAST_CHUNKS)
    return jnp.where(jnp.asarray(ok)[None], tab, NEG)


def _mixer_d_kernel(q_ref, k0_ref, k1_ref, v0_ref, v1_ref, sink_ref, o_ref):
    i = pl.program_id(1)
    k_refs = (k0_ref, k1_ref)
    v_refs = (v0_ref, v1_ref)
    qpos = TQ + lax.broadcasted_iota(jnp.int32, (TQ, TQ), 0)
    bias = []
    for a in range(2):
        kpos = a * TQ + lax.broadcasted_iota(jnp.int32, (TQ, TQ), 1)
        cdiff = qpos // CHUNK - kpos // CHUNK
        ok = (cdiff >= 0) & (cdiff <= D_WINDOW_CHUNKS)
        if a == 0:
            ok = ok & (i > 0)
        bias.append(jnp.where(ok, 0.0, NEG))
    group = N_HEADS // D_KV_HEADS

    def piece(h):
        n = (h // group) * 2 + h % 2
        return slice(n * LANES, (n + 1) * LANES)

    _window_heads(q_ref, o_ref, 2,
                  lambda h, a: k_refs[a][:, piece(h)],
                  lambda h, a: v_refs[a][:, piece(h)],
                  lambda h, a: bias[a],
                  sink_of=lambda h: sink_ref[h:h + 1, 0:1])


def _mixer_d(q, k, v, sinks):
    bsz, seq, _ = q.shape
    qmap = lambda b, i: (b, i, 0)
    prev = lambda b, i: (b, jnp.maximum(i - 1, 0), 0)
    sink_tab = jnp.broadcast_to((sinks.astype(F32) * LOG2E)[:, None], (N_HEADS, LANES))
    return pl.pallas_call(
        _mixer_d_kernel,
        grid=(bsz, seq // TQ),
        in_specs=[pl.BlockSpec((None, TQ, HW), qmap),
                  pl.BlockSpec((None, TQ, HW), prev), pl.BlockSpec((None, TQ, HW), qmap),
                  pl.BlockSpec((None, TQ, HW), prev), pl.BlockSpec((None, TQ, HW), qmap),
                  pl.BlockSpec((N_HEADS, LANES), lambda b, i: (0, 0))],
        out_specs=pl.BlockSpec((None, TQ, HW), qmap),
        out_shape=jax.ShapeDtypeStruct((bsz, seq, HW), BF16),
        compiler_params=_cparams(2),
        name="mixer_d",
    )(q, k, k, v, v, sink_tab)


def _out_proj_kernel(h_ref, y0_ref, y1_ref, w0_ref, w1_ref, o_ref):
    o_ref[...] = h_ref[...] + _dot(y0_ref[...], w0_ref[...]) + _dot(y1_ref[...], w1_ref[...])


def _out_proj(h2, y0, y1, w0, w1, seq):
    m = h2.shape[0]
    tm = min(TM_PROJ, seq)
    row = lambda i: (i, 0)
    const = lambda i: (0, 0)
    return pl.pallas_call(
        _out_proj_kernel,
        grid=(m // tm,),
        in_specs=[pl.BlockSpec((tm, D_MODEL), row),
                  pl.BlockSpec((tm, HW), row), pl.BlockSpec((tm, HW), row),
                  pl.BlockSpec((HW, D_MODEL), const), pl.BlockSpec((HW, D_MODEL), const)],
        out_specs=pl.BlockSpec((tm, D_MODEL), row),
        out_shape=jax.ShapeDtypeStruct((m, D_MODEL), F32),
        compiler_params=_cparams(1),
        name="out_proj",
    )(h2, y0, y1, w0, w1)


def _ffn_kernel(h_ref, halo_ref, g_ref, wg_ref, wu_ref, cwg_ref, cwu_ref, cbg_ref, cbu_ref,
                wd_ref, go_ref, o_ref, xn_ref, xh_ref, acc_ref, hb_ref, *, nseq, final_norm):
    i = pl.program_id(0)
    f = pl.program_id(1)
    tm = h_ref.shape[0]

    @pl.when(f == 0)
    def _():
        xn_ref[...] = _rms(h_ref[...], g_ref[...]).astype(BF16)
        xh_ref[...] = _rms(halo_ref[...], g_ref[...]).astype(BF16)
        acc_ref[...] = jnp.zeros_like(acc_ref)

    halo_on = jnp.where(i % nseq == 0, 0.0, 1.0)

    def branch(w_ref, cw_ref, cb_ref):
        hb_ref[HALO:, :] = _dot(xn_ref[...], w_ref[...])
        hb_ref[:HALO, :] = _dot(xh_ref[...], w_ref[...]) * halo_on
        cw = cw_ref[...]
        return (hb_ref[pl.ds(HALO, tm), :] * cw[2:3, :]
                + hb_ref[pl.ds(HALO - 1, tm), :] * cw[1:2, :]
                + hb_ref[pl.ds(HALO - 2, tm), :] * cw[0:1, :]
                + cb_ref[...])

    yg = branch(wg_ref, cwg_ref, cbg_ref)
    yu = branch(wu_ref, cwu_ref, cbu_ref)
    act = (yg / (1.0 + jnp.exp(-yg))) * yu
    acc_ref[...] += _dot(act.astype(BF16), wd_ref[...])

    @pl.when(f == pl.num_programs(1) - 1)
    def _():
        out = h_ref[...] + acc_ref[...]
        if final_norm:
            out = _rms(out, go_ref[...])
        o_ref[...] = out


def _ffn(h2, g, w_up, conv_w, conv_b, w_down, g_out, seq, final_norm):
    m = h2.shape[0]
    tm = min(TM_FFN, seq)
    nseq = seq // tm
    nf = D_FF // TF
    row = lambda i, f: (i, 0)
    const = lambda i, f: (0, 0)
    halo = lambda i, f: (jnp.maximum(i * (tm // HALO) - 1, 0), 0)
    gcol = lambda i, f: (0, f)
    ucol = lambda i, f: (0, nf + f)
    return pl.pallas_call(
        functools.partial(_ffn_kernel, nseq=nseq, final_norm=final_norm),
        grid=(m // tm, nf),
        in_specs=[pl.BlockSpec((tm, D_MODEL), row),
                  pl.BlockSpec((HALO, D_MODEL), halo),
                  pl.BlockSpec((1, D_MODEL), const),
                  pl.BlockSpec((D_MODEL, TF), gcol), pl.BlockSpec((D_MODEL, TF), ucol),
                  pl.BlockSpec((3, TF), gcol), pl.BlockSpec((3, TF), ucol),
                  pl.BlockSpec((1, TF), gcol), pl.BlockSpec((1, TF), ucol),
                  pl.BlockSpec((TF, D_MODEL), lambda i, f: (f, 0)),
                  pl.BlockSpec((1, D_MODEL), const)],
        out_specs=pl.BlockSpec((tm, D_MODEL), row),
        out_shape=jax.ShapeDtypeStruct((m, D_MODEL), F32),
        scratch_shapes=[pltpu.VMEM((tm, D_MODEL), BF16),
                        pltpu.VMEM((HALO, D_MODEL), BF16),
                        pltpu.VMEM((tm, D_MODEL), F32),
                        pltpu.VMEM((tm + HALO, TF), F32)],
        compiler_params=_cparams(2),
        name="ffn",
    )(h2, h2, g, w_up, w_up, conv_w, conv_w, conv_b, conv_b, w_down, g_out)


def _rope_tables(seq):
    inv = 1.0 / (ROPE_THETA ** (jnp.arange(0, HEAD_DIM, 2, dtype=F32) / HEAD_DIM))
    ang = jnp.arange(seq, dtype=F32)[:, None] * inv[None, :]
    c, s = jnp.cos(ang), jnp.sin(ang)
    cos = jnp.concatenate([c, c, c, c], axis=1)
    sin = jnp.concatenate([-s, s, -s, s], axis=1)
    return cos, sin


def _pad_cols(w, width):
    return jnp.pad(w, ((0, 0), (0, width - w.shape[1])))


def _even_weights(w_in):
    o = np.cumsum([0, 512, 512, 512, IDX_HEADS * IDX_DIM, IDX_DIM, IDX_HEADS, 512, 512, 512])
    sl = [w_in[:, o[n]:o[n + 1]] for n in range(9)]
    small = jnp.concatenate([_pad_cols(sl[4], LANES), _pad_cols(sl[5], LANES)], axis=1)
    return jnp.concatenate([sl[0], sl[1], sl[2], sl[3], small, sl[6], sl[7], sl[8]],
                           axis=1).astype(BF16)


def _odd_weights(w_in):
    o = np.cumsum([0, 512, 512, 512, N_HEADS, 512, D_KV_HEADS * HEAD_DIM, D_KV_HEADS * HEAD_DIM])
    sl = [w_in[:, o[n]:o[n + 1]] for n in range(7)]
    return jnp.concatenate([sl[0], sl[1], sl[2], _pad_cols(sl[3], LANES), sl[4], sl[5], sl[6]],
                           axis=1).astype(BF16)


def _row(v, width=None):
    v = v.astype(F32)[None, :]
    return v if width is None else _pad_cols(v, width)


def kernel(x, norm_mix_g, norm_ffn_g, norm_out_g, even_w_in, even_w_out, idx_k_ln_g, idx_k_ln_b,
           rel_bias, odd_w_in, odd_w_out, forget_b, sinks, ffn_w_up, ffn_conv_w, ffn_conv_b,
           ffn_w_down):
    bsz, seq, _ = x.shape
    m = bsz * seq
    depth = norm_mix_g.shape[0]
    assert seq % TQ == 0 and seq % min(TM_PROJ, seq) == 0 and seq % min(TKB_MAX, seq) == 0
    cos, sin = _rope_tables(seq)
    h = x.reshape(m, D_MODEL)
    as3 = lambda t: t.reshape(bsz, seq, t.shape[-1])

    for layer in range(depth):
        jj = layer // 2
        g_mix = _row(norm_mix_g[layer])
        if layer % 2 == 0:
            qa, ka, va, qi, ki, wi, qb, kb, vb = _even_proj(
                h, g_mix, _even_weights(even_w_in[jj]), cos, sin,
                _row(idx_k_ln_g[jj], LANES), _row(idx_k_ln_b[jj], LANES), seq)
            kit = jnp.swapaxes(as3(ki), 1, 2)
            kie, kio = kit, jnp.roll(kit, IDX_DIM, axis=1)
            y0 = _mixer_a(as3(qi), as3(wi), kie, kio, as3(qa), as3(ka), as3(va))
            y1 = _mixer_b(as3(qb), as3(kb), as3(vb), _b_bias_table(rel_bias[jj]))
            w_out = even_w_out[jj]
        else:
            qc, kc, vc, fcum, qd, kd, vd = _odd_proj(
                h, g_mix, _odd_weights(odd_w_in[jj]), cos, sin,
                _row(forget_b[jj], LANES), seq)
            fq = as3(fcum)
            fk = jnp.swapaxes(fq[:, :, :N_HEADS], 1, 2)
            y0 = _mixer_c(as3(qc), as3(kc), as3(vc), fq, fk)
            y1 = _mixer_d(as3(qd), as3(kd), as3(vd), sinks[jj])
            w_out = odd_w_out[jj]
        w_out = w_out.astype(BF16)
        h = _out_proj(h, y0.reshape(m, HW), y1.reshape(m, HW), w_out[:HW], w_out[HW:], seq)
        h = _ffn(h, _row(norm_ffn_g[layer]), ffn_w_up[layer].astype(BF16),
                 ffn_conv_w[layer].astype(F32), _row(ffn_conv_b[layer]),
                 ffn_w_down[layer].astype(BF16), _row(norm_out_g), seq,
                 final_norm=(layer == depth - 1))
    return h.reshape(bsz, seq, D_MODEL)
```

```python
import functools

import numpy as np
import jax
import jax.numpy as jnp
from jax import lax
from jax.experimental import pallas as pl
from jax.experimental.pallas import tpu as pltpu

F32 = jnp.float32
BF16 = jnp.bfloat16

D_MODEL = 1024
HEAD_DIM = 64
CHUNK = 64
ROPE_THETA = 10000.0
EPS = 1e-6
NEG = -1e30
N_HEADS = 8
IDX_HEADS = 4
IDX_DIM = 64
IDX_TOPK_MAX = 256
B_PAST_CHUNKS = 8
B_REL_CLIP = 256
D_KV_HEADS = 2
D_WINDOW_CHUNKS = 2
D_FF = 2816
HW = N_HEADS * HEAD_DIM
SCALE = HEAD_DIM ** -0.5
LOG2E = 1.4426950408889634
QSCALE = SCALE * LOG2E

LANES = 128
VMEM_LIMIT = 56 * 1024 * 1024

TM_PROJ = 512
TM_FFN = 1024
TQ = 256
TQ_C = 256
SUB = 512
TKB_MAX = 2048
TF = 256
HALO = 16
LAZY_MARGIN = 20.0


def _cparams(n_axes):
    return pltpu.CompilerParams(
        dimension_semantics=("arbitrary",) * n_axes,
        vmem_limit_bytes=VMEM_LIMIT)


def _rms(x, g):
    ms = jnp.mean(x * x, axis=-1, keepdims=True)
    return x * lax.rsqrt(ms + EPS) * g


def _lane_iota(shape):
    return lax.broadcasted_iota(jnp.int32, shape, len(shape) - 1)


def _rope(t, cos, sin):
    w = t.shape[1]
    lane = _lane_iota(t.shape)
    fwd = pltpu.roll(t, 32, 1)
    bwd = pltpu.roll(t, w - 32, 1)
    rot = jnp.where((lane % HEAD_DIM) < (HEAD_DIM // 2), bwd, fwd)
    return t * cos + rot * sin


def _tile_lanes(t, reps):
    return t if reps == 1 else jnp.concatenate([t] * reps, axis=1)


def _head_pieces(p, sw=None):
    lane = _lane_iota(p.shape)
    lo = lane < HEAD_DIM
    q = p if sw is None else sw
    even = jnp.where(lane == HEAD_DIM, 1.0, jnp.where(lo, p, 0.0))
    odd = jnp.where(lane == 0, 1.0, jnp.where(lo, 0.0, q))
    return even, odd


def _expand_heads(t, ones_lane=False):
    pieces = []
    for j in range(t.shape[1] // LANES):
        p = t[:, j * LANES:(j + 1) * LANES]
        if ones_lane:
            pieces.extend(_head_pieces(p))
        else:
            lo = _lane_iota(p.shape) < HEAD_DIM
            pieces.append(jnp.where(lo, p, 0.0))
            pieces.append(jnp.where(lo, 0.0, p))
    return jnp.concatenate(pieces, axis=1).astype(BF16)


def _expand_kv_pair(t, ones_lane=False):
    lo = _lane_iota(t.shape) < HEAD_DIM
    sw = pltpu.roll(t, HEAD_DIM, 1)
    if ones_lane:
        e0, o0 = _head_pieces(t, sw)
        e1, o1 = _head_pieces(sw, t)
        pieces = [e0, o0, e1, o1]
    else:
        pieces = [jnp.where(lo, t, 0.0), jnp.where(lo, 0.0, sw),
                  jnp.where(lo, sw, 0.0), jnp.where(lo, 0.0, t)]
    return jnp.concatenate(pieces, axis=1).astype(BF16)


def _dot(a, b):
    return jnp.dot(a, b, preferred_element_type=F32)


def _dot_nt(a, b):
    return lax.dot_general(a, b, (((1,), (1,)), ((), ())), preferred_element_type=F32)


def _even_proj_kernel(x_ref, g_ref, w_ref, cos_ref, sin_ref, lng_ref, lnb_ref,
                      qa_ref, ka_ref, va_ref, qi_ref, ki_ref, wi_ref,
                      qb_ref, kb_ref, vb_ref):
    xn = _rms(x_ref[...], g_ref[...]).astype(BF16)
    cos = cos_ref[...]
    sin = sin_ref[...]
    cos4 = _tile_lanes(cos, 4)
    sin4 = _tile_lanes(sin, 4)

    def proj(a, b):
        return _dot(xn, w_ref[:, a:b])

    qa_ref[...] = (_rope(proj(0, 512), cos4, sin4) * QSCALE).astype(BF16)
    ka_ref[...] = _expand_heads(_rope(proj(512, 1024), cos4, sin4))
    va_ref[...] = _expand_heads(proj(1024, 1536), ones_lane=True)
    qi_ref[...] = _rope(proj(1536, 1792), _tile_lanes(cos, 2), _tile_lanes(sin, 2)).astype(BF16)
    small = proj(1792, 2048)
    kraw = small[:, :LANES]
    wi_ref[...] = small[:, LANES:]
    valid = _lane_iota(kraw.shape) < IDX_DIM
    mean = jnp.sum(kraw, axis=-1, keepdims=True) * (1.0 / IDX_DIM)
    xc = jnp.where(valid, kraw - mean, 0.0)
    var = jnp.sum(xc * xc, axis=-1, keepdims=True) * (1.0 / IDX_DIM)
    kn = xc * lax.rsqrt(var + EPS) * lng_ref[...] + lnb_ref[...]
    ki_ref[...] = _rope(kn, cos, sin).astype(BF16)
    qb_ref[...] = (proj(2048, 2560) * QSCALE).astype(BF16)
    kb_ref[...] = _expand_heads(proj(2560, 3072))
    vb_ref[...] = _expand_heads(proj(3072, 3584), ones_lane=True)


def _even_proj(x2, g, w, cos, sin, lng, lnb, seq):
    m = x2.shape[0]
    tm = min(TM_PROJ, seq)
    nseq = seq // tm
    row = lambda i: (i, 0)
    const = lambda i: (0, 0)
    tab = lambda i: (i % nseq, 0)
    widths = [(512, BF16), (1024, BF16), (1024, BF16), (256, BF16), (128, BF16),
              (128, F32), (512, BF16), (1024, BF16), (1024, BF16)]
    return pl.pallas_call(
        _even_proj_kernel,
        grid=(m // tm,),
        in_specs=[pl.BlockSpec((tm, D_MODEL), row),
                  pl.BlockSpec((1, D_MODEL), const),
                  pl.BlockSpec(w.shape, const),
                  pl.BlockSpec((tm, LANES), tab),
                  pl.BlockSpec((tm, LANES), tab),
                  pl.BlockSpec((1, LANES), const),
                  pl.BlockSpec((1, LANES), const)],
        out_specs=[pl.BlockSpec((tm, wd), row) for wd, _ in widths],
        out_shape=[jax.ShapeDtypeStruct((m, wd), dt) for wd, dt in widths],
        compiler_params=_cparams(1),
        name="even_proj",
    )(x2, g, w, cos, sin, lng, lnb)


def _odd_proj_kernel(x_ref, g_ref, w_ref, cos_ref, sin_ref, fb_ref, tril_ref,
                     qc_ref, kc_ref, vc_ref, f_ref, qd_ref, kd_ref, vd_ref,
                     carry_ref, *, nseq):
    i = pl.program_id(0)
    xn = _rms(x_ref[...], g_ref[...]).astype(BF16)
    cos = cos_ref[...]
    sin = sin_ref[...]

    def proj(a, b):
        return _dot(xn, w_ref[:, a:b])

    qc_ref[...] = (proj(0, 512) * QSCALE).astype(BF16)
    kc_ref[...] = _expand_heads(proj(512, 1024))
    vc_ref[...] = _expand_heads(proj(1024, 1536), ones_lane=True)

    z = proj(1536, 1664) + fb_ref[...]
    lf = jnp.minimum(z, 0.0) - jnp.log(1.0 + jnp.exp(-jnp.abs(z)))
    lf = jnp.where(_lane_iota(lf.shape) < N_HEADS, lf, 0.0)
    hi = lf.astype(BF16)
    r1 = lf - hi.astype(F32)
    mid = r1.astype(BF16)
    lo = (r1 - mid.astype(F32)).astype(BF16)
    tril = tril_ref[...]
    csum = _dot(tril, hi) + _dot(tril, mid) + _dot(tril, lo)

    @pl.when(i % nseq == 0)
    def _():
        carry_ref[...] = jnp.zeros_like(carry_ref)

    fcum = csum + carry_ref[0:1, :]
    f_ref[...] = fcum * LOG2E
    carry_ref[...] = jnp.broadcast_to(fcum[fcum.shape[0] - 1:, :], carry_ref.shape)

    qd_ref[...] = (_rope(proj(1664, 2176), _tile_lanes(cos, 4), _tile_lanes(sin, 4))
                   * QSCALE).astype(BF16)
    kv = proj(2176, 2432)
    kd_ref[...] = _expand_kv_pair(_rope(kv[:, :LANES], cos, sin))
    vd_ref[...] = _expand_kv_pair(kv[:, LANES:], ones_lane=True)


def _odd_proj(x2, g, w, cos, sin, fb, seq):
    m = x2.shape[0]
    tm = min(TM_PROJ, seq)
    nseq = seq // tm
    row = lambda i: (i, 0)
    const = lambda i: (0, 0)
    tab = lambda i: (i % nseq, 0)
    tril = jnp.tril(jnp.ones((tm, tm), F32)).astype(BF16)
    widths = [(512, BF16), (1024, BF16), (1024, BF16), (128, F32),
              (512, BF16), (512, BF16), (512, BF16)]
    return pl.pallas_call(
        functools.partial(_odd_proj_kernel, nseq=nseq),
        grid=(m // tm,),
        in_specs=[pl.BlockSpec((tm, D_MODEL), row),
                  pl.BlockSpec((1, D_MODEL), const),
                  pl.BlockSpec(w.shape, const),
                  pl.BlockSpec((tm, LANES), tab),
                  pl.BlockSpec((tm, LANES), tab),
                  pl.BlockSpec((1, LANES), const),
                  pl.BlockSpec((tm, tm), const)],
        out_specs=[pl.BlockSpec((tm, wd), row) for wd, _ in widths],
        out_shape=[jax.ShapeDtypeStruct((m, wd), dt) for wd, dt in widths],
        scratch_shapes=[pltpu.VMEM((8, LANES), F32)],
        compiler_params=_cparams(1),
        name="odd_proj",
    )(x2, g, w, cos, sin, fb, tril)


def _denominator_lane(h):
    return HEAD_DIM if h % 2 == 0 else 0


def _flash_heads(q_ref, k_ref, v_ref, rows, acc_ref, m_ref, logit_fn):
    nrep = SUB // LANES
    s_next = _head_scores(q_ref, k_ref, rows, 0)
    for h in range(N_HEADS):
        s_raw = s_next
        if h + 1 < N_HEADS:
            s_next = _head_scores(q_ref, k_ref, rows, h + 1)
        s, shift = logit_fn(h, s_raw)
        m_prev = m_ref[h]
        m_cur = jnp.max(s, axis=1, keepdims=True)
        if shift is not None:
            m_cur = m_cur + shift
        m_new = jnp.maximum(m_prev, m_cur)
        alpha = jnp.exp2(m_prev - m_new)
        off = m_new if shift is None else m_new - shift
        p = jnp.exp2(s - _tile_lanes(off, nrep))
        m_ref[h] = m_new
        cols = slice(h * LANES, (h + 1) * LANES)
        acc_ref[:, cols] = acc_ref[:, cols] * alpha + _dot(p.astype(BF16), v_ref[rows, cols])


def _head_scores(q_ref, k_ref, rows, h):
    jp = h // 2
    return _dot_nt(q_ref[:, jp * LANES:(jp + 1) * LANES], k_ref[rows, h * LANES:(h + 1) * LANES])


def _flash_heads_lazy(q_ref, k_ref, v_ref, rows, acc_ref, m_ref, tmp_ref, logit_fn):
    nrep = SUB // LANES
    worst = None
    pending = [_head_scores(q_ref, k_ref, rows, h) for h in range(2)]
    for h in range(N_HEADS):
        if h + 2 < N_HEADS:
            pending.append(_head_scores(q_ref, k_ref, rows, h + 2))
        s, shift = logit_fn(h, pending.pop(0))
        ref = m_ref[h] if shift is None else m_ref[h] - shift
        lane_max = functools.reduce(
            jnp.maximum, [s[:, t * LANES:(t + 1) * LANES] for t in range(nrep)])
        over = lane_max - ref
        worst = over if worst is None else jnp.maximum(worst, over)
        p = jnp.exp2(s - _tile_lanes(ref, nrep))
        cols = slice(h * LANES, (h + 1) * LANES)
        tmp_ref[:, cols] = _dot(p.astype(BF16), v_ref[rows, cols])
    exceeded = jnp.max(worst) > LAZY_MARGIN

    @pl.when(jnp.logical_not(exceeded))
    def _():
        acc_ref[...] += tmp_ref[...]

    @pl.when(exceeded)
    def _():
        _flash_heads(q_ref, k_ref, v_ref, rows, acc_ref, m_ref, logit_fn)


def _flash_init(acc_ref, m_ref):
    acc_ref[...] = jnp.zeros_like(acc_ref)
    m_ref[...] = jnp.full(m_ref.shape, NEG, F32)


def _pair_output(even, odd, l_even, l_odd):
    lo = _lane_iota(even.shape) < HEAD_DIM
    return jnp.where(lo, even / l_even, odd / l_odd)


def _flash_finish(o_ref, acc_ref):
    for jp in range(N_HEADS // 2):
        even = acc_ref[:, (2 * jp) * LANES:(2 * jp + 1) * LANES]
        odd = acc_ref[:, (2 * jp + 1) * LANES:(2 * jp + 2) * LANES]
        le = even[:, _denominator_lane(0):_denominator_lane(0) + 1]
        lo_ = odd[:, _denominator_lane(1):_denominator_lane(1) + 1]
        o_ref[:, jp * LANES:(jp + 1) * LANES] = _pair_output(even, odd, le, lo_).astype(o_ref.dtype)


def _kv_tiles(seq):
    tkb = min(TKB_MAX, seq)
    return tkb, seq // tkb, tkb // SUB


def _mixer_c_kernel(q_ref, k_ref, v_ref, fq_ref, fk_ref, o_ref, acc_ref, m_ref, fqr_ref,
                    *, tq, tkb, nsub):
    i = pl.program_id(1)
    j = pl.program_id(2)
    q_lo = i * tq
    last_sub = (q_lo + tq - 1) // SUB
    first_masked = q_lo // SUB
    jmax = last_sub // nsub

    @pl.when(j == 0)
    def _():
        _flash_init(acc_ref, m_ref)
        for h in range(N_HEADS):
            fqr_ref[h] = jnp.broadcast_to(fq_ref[:, h:h + 1], (tq, LANES))

    def step(c, masked):
        rows = pl.ds(pl.multiple_of(c * SUB, SUB), SUB)

        def logit_fn(h, s):
            s = s - fk_ref[h:h + 1, rows]
            if masked:
                kpos = j * tkb + c * SUB + lax.broadcasted_iota(jnp.int32, s.shape, 1)
                qpos = q_lo + lax.broadcasted_iota(jnp.int32, s.shape, 0)
                s = jnp.where(kpos <= qpos, s, NEG)
            return s, fqr_ref[h]

        _flash_heads(q_ref, k_ref, v_ref, rows, acc_ref, m_ref, logit_fn)

    @pl.when(j <= jmax)
    def _():
        lo = j * nsub
        n_full = jnp.clip(first_masked - lo, 0, nsub)
        n_all = jnp.clip(last_sub + 1 - lo, 0, nsub)

        def body(c, carry):
            step(c, False)
            return carry

        def masked_body(c, carry):
            step(c, True)
            return carry

        lax.fori_loop(0, n_full, body, 0)
        lax.fori_loop(n_full, n_all, masked_body, 0)

        @pl.when(j == jmax)
        def _():
            _flash_finish(o_ref, acc_ref)


def _mixer_c(q, k, v, fq, fk):
    bsz, seq, _ = q.shape
    tq = min(TQ_C, seq)
    tkb, nkb, nsub = _kv_tiles(seq)
    kv_idx = lambda b, i, j: (b, jnp.minimum(j, (i * tq + tq - 1) // tkb), 0)
    return pl.pallas_call(
        functools.partial(_mixer_c_kernel, tq=tq, tkb=tkb, nsub=nsub),
        grid=(bsz, seq // tq, nkb),
        in_specs=[pl.BlockSpec((None, tq, HW), lambda b, i, j: (b, i, 0)),
                  pl.BlockSpec((None, tkb, 2 * HW), kv_idx),
                  pl.BlockSpec((None, tkb, 2 * HW), kv_idx),
                  pl.BlockSpec((None, tq, LANES), lambda b, i, j: (b, i, 0)),
                  pl.BlockSpec((None, N_HEADS, tkb),
                               lambda b, i, j: (b, 0, jnp.minimum(j, (i * tq + tq - 1) // tkb)))],
        out_specs=pl.BlockSpec((None, tq, HW), lambda b, i, j: (b, i, 0)),
        out_shape=jax.ShapeDtypeStruct((bsz, seq, HW), BF16),
        scratch_shapes=[pltpu.VMEM((tq, 2 * HW), F32),
                        pltpu.VMEM((N_HEADS, tq, LANES), F32),
                        pltpu.VMEM((N_HEADS, tq, LANES), F32)],
        compiler_params=_cparams(3),
        name="mixer_c",
    )(q, k, v, fq, fk)


MIN_NORMAL_KEY = 0x00800000
BLIND_PASSES = 10
COUNT_UNROLL = 4
PASS_UNROLL = 4


def _key_flip(bits):
    return jnp.where(bits >= 0, bits, bits ^ jnp.int32(0x7FFFFFFF))


def _float_to_key(f):
    return _key_flip(pltpu.bitcast(f, jnp.int32))


def _key_to_float(k):
    return pltpu.bitcast(_key_flip(k), F32)


def _mixer_a_kernel(qi_ref, wi_ref, kie_ref, kio_ref, q_ref, k_ref, v_ref, tri_ref,
                    o_ref, s_ref, gmax_ref, cnt_ref, acc_ref, m_ref, tmp_ref,
                    *, tkb, nsub, topk):
    i = pl.program_id(1)
    j = pl.program_id(2)
    q_lo = i * TQ
    last_sub = q_lo // SUB
    jmax = last_sub // nsub
    n_sub = last_sub + 1
    nrep = SUB // LANES
    half = TQ // 2

    def admissible(c_glob, shape):
        kpos = c_glob * SUB + lax.broadcasted_iota(jnp.int32, shape, 1)
        qpos = q_lo + lax.broadcasted_iota(jnp.int32, shape, 0)
        return (kpos // CHUNK) <= (qpos // CHUNK)

    def s_load(c):
        return jnp.concatenate([s_ref[c * nrep + t] for t in range(nrep)], axis=1)

    def s_store(c, val):
        for t in range(nrep):
            s_ref[c * nrep + t] = val[:, t * LANES:(t + 1) * LANES]

    def masked_logits(c_glob):
        return lambda h, s: (s + s_load(c_glob), None)

    @pl.when(j == 0)
    def _():
        _flash_init(acc_ref, m_ref)
        cnt_ref[...] = jnp.zeros_like(cnt_ref)
        gmax_ref[...] = jnp.full(gmax_ref.shape, -jnp.inf, F32)

        def score(c, masked):
            cols = pl.ds(pl.multiple_of(c * SUB, SUB), SUB)
            tot = None
            for pair in range(IDX_HEADS // 2):
                qp = qi_ref[:, pair * LANES:(pair + 1) * LANES]
                for e in range(2):
                    h = 2 * pair + e
                    kt = (kie_ref if e == 0 else kio_ref)[:, cols]
                    term = jnp.maximum(_dot(qp, kt), 0.0) * wi_ref[:, h:h + 1]
                    tot = term if tot is None else tot + term
            if masked:
                tot = jnp.where(admissible(c, tot.shape), tot, NEG)
            s_store(c, tot)
            for par in range(2):
                g = gmax_ref[par]
                for t in range(par, nrep, 2):
                    g = jnp.maximum(g, tot[:, t * LANES:(t + 1) * LANES])
                gmax_ref[par] = g

        def for_each_full_sub_tile(fn):
            u = PASS_UNROLL

            def group(g, carry):
                for n in range(u):
                    fn(g * u + n)
                return carry

            def single(c, carry):
                fn(c)
                return carry

            lax.fori_loop(0, last_sub // u, group, 0)
            lax.fori_loop(u * (last_sub // u), last_sub, single, 0)

        for_each_full_sub_tile(lambda c: score(c, False))
        score(last_sub, True)

        def count(cand):
            accs = []
            for hh in range(2):
                r0 = hh * half
                ch = cand[r0:r0 + half]

                def scan(t0, width, acc):
                    for t in range(width):
                        blk = s_ref[t0 + t, r0:r0 + half, :]
                        acc = acc + jnp.where(blk >= ch, 1.0, 0.0)
                    return acc

                u = COUNT_UNROLL
                acc = lax.fori_loop(0, n_sub // u, lambda c, a: scan(c * u * nrep, u * nrep, a),
                                    jnp.zeros((half, LANES), F32))
                acc = lax.fori_loop(u * (n_sub // u), n_sub, lambda c, a: scan(c * nrep, nrep, a), acc)
                accs.append(acc)
            total = jnp.sum(jnp.concatenate(accs, axis=0), axis=1, keepdims=True)
            return jnp.broadcast_to(total, (TQ, LANES))

        ga = gmax_ref[0]
        gb = gmax_ref[1]
        lo_f = jnp.min(jnp.minimum(ga, gb), axis=1, keepdims=True)
        hi_f = jnp.max(jnp.maximum(ga, gb), axis=1, keepdims=True)
        lo0 = jnp.broadcast_to(_float_to_key(lo_f), (TQ, LANES))
        hi0 = jnp.broadcast_to(_float_to_key(hi_f), (TQ, LANES))

        def unresolved(st):
            lo, hi, _ = st
            return jnp.max(jnp.where(lo < hi, 1.0, 0.0)) > 0.5

        def narrow(st, mid):
            lo, hi, n_above = st
            n_ge = count(_key_to_float(mid))
            ge = n_ge >= topk
            exact = n_ge == topk
            below = jnp.where(mid == MIN_NORMAL_KEY, 0, mid - 1)
            return (jnp.where(ge, mid, lo),
                    jnp.where(exact, mid, jnp.where(ge, hi, below)),
                    jnp.where(exact, 0.0, jnp.where(ge, n_above, n_ge)))

        def bisect(st):
            lo, hi, _ = st
            by_key = (lo >> 1) + (hi >> 1) + ((lo | hi) & 1)
            lo_f = _key_to_float(lo)
            hi_f = _key_to_float(hi)
            by_val = _float_to_key(lo_f + (hi_f - lo_f) * 0.5)
            by_val = jnp.minimum(jnp.maximum(by_val, lo + 1), hi)
            same_binade = (pltpu.bitcast(lo_f, jnp.int32) >> 23) == (pltpu.bitcast(hi_f, jnp.int32) >> 23)
            mid = jnp.where(same_binade, by_key, by_val)
            mid = jnp.where(lo < 0, jnp.where(hi >= 0, 0, mid), mid)
            mid = jnp.where(lo == 0, jnp.where(hi >= MIN_NORMAL_KEY, MIN_NORMAL_KEY, mid), mid)
            mid = jnp.where(lo < hi, mid, lo)
            return narrow(st, mid)

        state = narrow((lo0, hi0, jnp.zeros((TQ, LANES), F32)), hi0)
        state = lax.fori_loop(0, BLIND_PASSES, lambda _, st: bisect(st), state)
        lo, _, n_above = lax.while_loop(unresolved, bisect, state)
        tau = _tile_lanes(_key_to_float(lo), nrep)
        keep = _tile_lanes(topk - n_above, nrep)

        def to_mask(c, masked):
            blk = s_load(c)
            eq = blk == tau
            rank = (_tile_lanes(cnt_ref[...], nrep)
                    + _dot(jnp.where(eq, 1.0, 0.0).astype(BF16), tri_ref[...]))
            keep_tie = jnp.where(rank <= keep, 0.0, NEG)
            bias = jnp.where(blk > tau, 0.0, jnp.where(eq, keep_tie, NEG))
            if masked:
                bias = jnp.where(admissible(c, bias.shape), bias, NEG)
            cnt_ref[...] = jnp.broadcast_to(rank[:, SUB - 1:SUB], (TQ, LANES))
            s_store(c, bias)

        for_each_full_sub_tile(lambda c: to_mask(c, False))
        to_mask(last_sub, True)

        _flash_heads(q_ref, k_ref, v_ref, pl.ds(0, SUB), acc_ref, m_ref, masked_logits(0))

    @pl.when(j <= jmax)
    def _():
        def body(c, carry):
            rows = pl.ds(pl.multiple_of(c * SUB, SUB), SUB)
            _flash_heads_lazy(q_ref, k_ref, v_ref, rows, acc_ref, m_ref, tmp_ref,
                              masked_logits(j * nsub + c))
            return carry

        lax.fori_loop(jnp.where(j == 0, 1, 0), jnp.minimum(n_sub - j * nsub, nsub), body, 0)

        @pl.when(j == jmax)
        def _():
            _flash_finish(o_ref, acc_ref)


def _mixer_a(qi, wi, kie, kio, q, k, v):
    bsz, seq, _ = q.shape
    tkb, nkb, nsub = _kv_tiles(seq)
    topk = min(IDX_TOPK_MAX, seq // 4)
    tri = jnp.triu(jnp.ones((SUB, SUB), F32)).astype(BF16)
    qmap = lambda b, i, j: (b, i, 0)
    kv_idx = lambda b, i, j: (b, jnp.minimum(j, (i * TQ) // tkb), 0)
    res = lambda b, i, j: (b, 0, 0)
    return pl.pallas_call(
        functools.partial(_mixer_a_kernel, tkb=tkb, nsub=nsub, topk=topk),
        grid=(bsz, seq // TQ, nkb),
        in_specs=[pl.BlockSpec((None, TQ, IDX_HEADS * IDX_DIM), qmap),
                  pl.BlockSpec((None, TQ, LANES), qmap),
                  pl.BlockSpec((None, LANES, seq), res),
                  pl.BlockSpec((None, LANES, seq), res),
                  pl.BlockSpec((None, TQ, HW), qmap),
                  pl.BlockSpec((None, tkb, 2 * HW), kv_idx),
                  pl.BlockSpec((None, tkb, 2 * HW), kv_idx),
                  pl.BlockSpec((SUB, SUB), lambda b, i, j: (0, 0))],
        out_specs=pl.BlockSpec((None, TQ, HW), qmap),
        out_shape=jax.ShapeDtypeStruct((bsz, seq, HW), BF16),
        scratch_shapes=[pltpu.VMEM((seq // LANES, TQ, LANES), F32),
                        pltpu.VMEM((2, TQ, LANES), F32),
                        pltpu.VMEM((TQ, LANES), F32),
                        pltpu.VMEM((TQ, 2 * HW), F32),
                        pltpu.VMEM((N_HEADS, TQ, LANES), F32),
                        pltpu.VMEM((TQ, 2 * HW), F32)],
        compiler_params=_cparams(3),
        name="mixer_a",
    )(qi, wi, kie, kio, q, k, v, tri)


B_WIN = 3


def _window_heads(q_ref, o_ref, n_win, k_of, v_of, bias_of, sink_of=None):
    def scores(h):
        jp = h // 2
        qp = q_ref[:, jp * LANES:(jp + 1) * LANES]
        return [_dot_nt(qp, k_of(h, a)) + bias_of(h, a) for a in range(n_win)]

    s_next = scores(0)
    done = []
    for h in range(N_HEADS):
        s = s_next
        if h + 1 < N_HEADS:
            s_next = scores(h + 1)
        lane_max = functools.reduce(
            jnp.maximum, [t[:, n * LANES:(n + 1) * LANES] for t in s for n in range(t.shape[1] // LANES)])
        m = jnp.max(lane_max, axis=1, keepdims=True)
        if sink_of is not None:
            m = jnp.maximum(m, sink_of(h))
        m_rep = jnp.broadcast_to(m, lane_max.shape)
        pv = functools.reduce(
            jnp.add, [_dot(jnp.exp2(s[a] - _tile_lanes(m_rep, s[a].shape[1] // LANES)).astype(BF16),
                           v_of(h, a)) for a in range(n_win)])
        lane = _denominator_lane(h)
        l = pv[:, lane:lane + 1]
        if sink_of is not None:
            l = l + jnp.exp2(sink_of(h) - m)
        done.append((pv, l))
        if h % 2 == 1:
            (pe, le), (po, lo_) = done[-2], done[-1]
            jp = h // 2
            o_ref[:, jp * LANES:(jp + 1) * LANES] = _pair_output(pe, po, le, lo_).astype(o_ref.dtype)


def _mixer_b_kernel(q_ref, k0_ref, k1_ref, k2_ref, v0_ref, v1_ref, v2_ref, bias_ref, o_ref):
    i = pl.program_id(1)
    k_refs = (k0_ref, k1_ref, k2_ref)
    v_refs = (v0_ref, v1_ref, v2_ref)
    pad = [jnp.where(i - (B_WIN - 1 - a) >= 0, 0.0, NEG) for a in range(B_WIN)]
    hc = lambda h: slice(h * LANES, (h + 1) * LANES)
    _window_heads(q_ref, o_ref, B_WIN,
                  lambda h, a: k_refs[a][:, hc(h)],
                  lambda h, a: v_refs[a][:, hc(h)],
                  lambda h, a: bias_ref[h, :, a * TQ:(a + 1) * TQ] + pad[a])


def _mixer_b(q, k, v, bias_tab):
    bsz, seq, _ = q.shape
    qmap = lambda b, i: (b, i, 0)
    kmaps = [functools.partial(lambda b, i, d: (b, jnp.maximum(i - d, 0), 0), d=B_WIN - 1 - a)
             for a in range(B_WIN)]
    kspec = [pl.BlockSpec((None, TQ, 2 * HW), km) for km in kmaps]
    return pl.pallas_call(
        _mixer_b_kernel,
        grid=(bsz, seq // TQ),
        in_specs=[pl.BlockSpec((None, TQ, HW), qmap)] + kspec + kspec
                 + [pl.BlockSpec(bias_tab.shape, lambda b, i: (0, 0, 0))],
        out_specs=pl.BlockSpec((None, TQ, HW), qmap),
        out_shape=jax.ShapeDtypeStruct((bsz, seq, HW), BF16),
        compiler_params=_cparams(2),
        name="mixer_b",
    )(q, k, k, k, v, v, v, bias_tab)


def _b_bias_table(rel_bias):
    win = B_WIN * TQ
    span = TQ + win
    u = np.arange(span)
    u = np.where(u >= win, u - span, u)
    idx = np.clip((win - TQ) - u, -B_REL_CLIP, B_REL_CLIP) + B_REL_CLIP
    v = rel_bias.astype(F32)[:, idx] * LOG2E
    tab = jnp.tile(v, (1, TQ))[:, :TQ * (span - 1)].reshape(N_HEADS, TQ, span - 1)[:, :, :win]
    t = np.arange(TQ)[:, None]
    jk = np.arange(win)[None, :]
    cdiff = (t // CHUNK + (B_WIN - 1) * TQ // CHUNK) - jk // CHUNK
    ok = (cdiff >= 0) & (cdiff <= B_PAST_CHUNKS)
    return jnp.where(jnp.asarray(ok)[None], tab, NEG)


def _mixer_d_kernel(q_ref, k0_ref, k1_ref, v0_ref, v1_ref, sink_ref, o_ref):
    i = pl.program_id(1)
    k_refs = (k0_ref, k1_ref)
    v_refs = (v0_ref, v1_ref)
    qpos = TQ + lax.broadcasted_iota(jnp.int32, (TQ, TQ), 0)
    bias = []
    for a in range(2):
        kpos = a * TQ + lax.broadcasted_iota(jnp.int32, (TQ, TQ), 1)
        cdiff = qpos // CHUNK - kpos // CHUNK
        ok = (cdiff >= 0) & (cdiff <= D_WINDOW_CHUNKS)
        if a == 0:
            ok = ok & (i > 0)
        bias.append(jnp.where(ok, 0.0, NEG))
    group = N_HEADS // D_KV_HEADS

    def piece(h):
        n = (h // group) * 2 + h % 2
        return slice(n * LANES, (n + 1) * LANES)

    _window_heads(q_ref, o_ref, 2,
                  lambda h, a: k_refs[a][:, piece(h)],
                  lambda h, a: v_refs[a][:, piece(h)],
                  lambda h, a: bias[a],
                  sink_of=lambda h: sink_ref[h:h + 1, 0:1])


def _mixer_d(q, k, v, sinks):
    bsz, seq, _ = q.shape
    qmap = lambda b, i: (b, i, 0)
    prev = lambda b, i: (b, jnp.maximum(i - 1, 0), 0)
    sink_tab = jnp.broadcast_to((sinks.astype(F32) * LOG2E)[:, None], (N_HEADS, LANES))
    return pl.pallas_call(
        _mixer_d_kernel,
        grid=(bsz, seq // TQ),
        in_specs=[pl.BlockSpec((None, TQ, HW), qmap),
                  pl.BlockSpec((None, TQ, HW), prev), pl.BlockSpec((None, TQ, HW), qmap),
                  pl.BlockSpec((None, TQ, HW), prev), pl.BlockSpec((None, TQ, HW), qmap),
                  pl.BlockSpec((N_HEADS, LANES), lambda b, i: (0, 0))],
        out_specs=pl.BlockSpec((None, TQ, HW), qmap),
        out_shape=jax.ShapeDtypeStruct((bsz, seq, HW), BF16),
        compiler_params=_cparams(2),
        name="mixer_d",
    )(q, k, k, v, v, sink_tab)


def _out_proj_kernel(h_ref, y0_ref, y1_ref, w0_ref, w1_ref, o_ref):
    o_ref[...] = h_ref[...] + _dot(y0_ref[...], w0_ref[...]) + _dot(y1_ref[...], w1_ref[...])


def _out_proj(h2, y0, y1, w0, w1, seq):
    m = h2.shape[0]
    tm = min(TM_PROJ, seq)
    row = lambda i: (i, 0)
    const = lambda i: (0, 0)
    return pl.pallas_call(
        _out_proj_kernel,
        grid=(m // tm,),
        in_specs=[pl.BlockSpec((tm, D_MODEL), row),
                  pl.BlockSpec((tm, HW), row), pl.BlockSpec((tm, HW), row),
                  pl.BlockSpec((HW, D_MODEL), const), pl.BlockSpec((HW, D_MODEL), const)],
        out_specs=pl.BlockSpec((tm, D_MODEL), row),
        out_shape=jax.ShapeDtypeStruct((m, D_MODEL), F32),
        compiler_params=_cparams(1),
        name="out_proj",
    )(h2, y0, y1, w0, w1)


def _ffn_kernel(h_ref, halo_ref, g_ref, wg_ref, wu_ref, cwg_ref, cwu_ref, cbg_ref, cbu_ref,
                wd_ref, go_ref, o_ref, xn_ref, xh_ref, acc_ref, hb_ref, *, nseq, final_norm):
    i = pl.program_id(0)
    f = pl.program_id(1)
    tm = h_ref.shape[0]

    @pl.when(f == 0)
    def _():
        xn_ref[...] = _rms(h_ref[...], g_ref[...]).astype(BF16)
        xh_ref[...] = _rms(halo_ref[...], g_ref[...]).astype(BF16)
        acc_ref[...] = jnp.zeros_like(acc_ref)

    halo_on = jnp.where(i % nseq == 0, 0.0, 1.0)

    def branch(w_ref, cw_ref, cb_ref):
        hb_ref[HALO:, :] = _dot(xn_ref[...], w_ref[...])
        hb_ref[:HALO, :] = _dot(xh_ref[...], w_ref[...]) * halo_on
        cw = cw_ref[...]
        return (hb_ref[pl.ds(HALO, tm), :] * cw[2:3, :]
                + hb_ref[pl.ds(HALO - 1, tm), :] * cw[1:2, :]
                + hb_ref[pl.ds(HALO - 2, tm), :] * cw[0:1, :]
                + cb_ref[...])

    yg = branch(wg_ref, cwg_ref, cbg_ref)
    yu = branch(wu_ref, cwu_ref, cbu_ref)
    act = (yg / (1.0 + jnp.exp(-yg))) * yu
    acc_ref[...] += _dot(act.astype(BF16), wd_ref[...])

    @pl.when(f == pl.num_programs(1) - 1)
    def _():
        out = h_ref[...] + acc_ref[...]
        if final_norm:
            out = _rms(out, go_ref[...])
        o_ref[...] = out


def _ffn(h2, g, w_up, conv_w, conv_b, w_down, g_out, seq, final_norm):
    m = h2.shape[0]
    tm = min(TM_FFN, seq)
    nseq = seq // tm
    nf = D_FF // TF
    row = lambda i, f: (i, 0)
    const = lambda i, f: (0, 0)
    halo = lambda i, f: (jnp.maximum(i * (tm // HALO) - 1, 0), 0)
    gcol = lambda i, f: (0, f)
    ucol = lambda i, f: (0, nf + f)
    return pl.pallas_call(
        functools.partial(_ffn_kernel, nseq=nseq, final_norm=final_norm),
        grid=(m // tm, nf),
        in_specs=[pl.BlockSpec((tm, D_MODEL), row),
                  pl.BlockSpec((HALO, D_MODEL), halo),
                  pl.BlockSpec((1, D_MODEL), const),
                  pl.BlockSpec((D_MODEL, TF), gcol), pl.BlockSpec((D_MODEL, TF), ucol),
                  pl.BlockSpec((3, TF), gcol), pl.BlockSpec((3, TF), ucol),
                  pl.BlockSpec((1, TF), gcol), pl.BlockSpec((1, TF), ucol),
                  pl.BlockSpec((TF, D_MODEL), lambda i, f: (f, 0)),
                  pl.BlockSpec((1, D_MODEL), const)],
        out_specs=pl.BlockSpec((tm, D_MODEL), row),
        out_shape=jax.ShapeDtypeStruct((m, D_MODEL), F32),
        scratch_shapes=[pltpu.VMEM((tm, D_MODEL), BF16),
                        pltpu.VMEM((HALO, D_MODEL), BF16),
                        pltpu.VMEM((tm, D_MODEL), F32),
                        pltpu.VMEM((tm + HALO, TF), F32)],
        compiler_params=_cparams(2),
        name="ffn",
    )(h2, h2, g, w_up, w_up, conv_w, conv_w, conv_b, conv_b, w_down, g_out)


def _rope_tables(seq):
    inv = 1.0 / (ROPE_THETA ** (jnp.arange(0, HEAD_DIM, 2, dtype=F32) / HEAD_DIM))
    ang = jnp.arange(seq, dtype=F32)[:, None] * inv[None, :]
    c, s = jnp.cos(ang), jnp.sin(ang)
    cos = jnp.concatenate([c, c, c, c], axis=1)
    sin = jnp.concatenate([-s, s, -s, s], axis=1)
    return cos, sin


def _pad_cols(w, width):
    return jnp.pad(w, ((0, 0), (0, width - w.shape[1])))


def _even_weights(w_in):
    o = np.cumsum([0, 512, 512, 512, IDX_HEADS * IDX_DIM, IDX_DIM, IDX_HEADS, 512, 512, 512])
    sl = [w_in[:, o[n]:o[n + 1]] for n in range(9)]
    small = jnp.concatenate([_pad_cols(sl[4], LANES), _pad_cols(sl[5], LANES)], axis=1)
    return jnp.concatenate([sl[0], sl[1], sl[2], sl[3], small, sl[6], sl[7], sl[8]],
                           axis=1).astype(BF16)


def _odd_weights(w_in):
    o = np.cumsum([0, 512, 512, 512, N_HEADS, 512, D_KV_HEADS * HEAD_DIM, D_KV_HEADS * HEAD_DIM])
    sl = [w_in[:, o[n]:o[n + 1]] for n in range(7)]
    return jnp.concatenate([sl[0], sl[1], sl[2], _pad_cols(sl[3], LANES), sl[4], sl[5], sl[6]],
                           axis=1).astype(BF16)


def _row(v, width=None):
    v = v.astype(F32)[None, :]
    return v if width is None else _pad_cols(v, width)


def kernel(x, norm_mix_g, norm_ffn_g, norm_out_g, even_w_in, even_w_out, idx_k_ln_g, idx_k_ln_b,
           rel_bias, odd_w_in, odd_w_out, forget_b, sinks, ffn_w_up, ffn_conv_w, ffn_conv_b,
           ffn_w_down):
    bsz, seq, _ = x.shape
    m = bsz * seq
    depth = norm_mix_g.shape[0]
    assert seq % TQ == 0 and seq % min(TM_PROJ, seq) == 0 and seq % min(TKB_MAX, seq) == 0
    cos, sin = _rope_tables(seq)
    h = x.reshape(m, D_MODEL)
    as3 = lambda t: t.reshape(bsz, seq, t.shape[-1])

    for layer in range(depth):
        jj = layer // 2
        g_mix = _row(norm_mix_g[layer])
        if layer % 2 == 0:
            qa, ka, va, qi, ki, wi, qb, kb, vb = _even_proj(
                h, g_mix, _even_weights(even_w_in[jj]), cos, sin,
                _row(idx_k_ln_g[jj], LANES), _row(idx_k_ln_b[jj], LANES), seq)
            kit = jnp.swapaxes(as3(ki), 1, 2)
            kie, kio = kit, jnp.roll(kit, IDX_DIM, axis=1)
            y0 = _mixer_a(as3(qi), as3(wi), kie, kio, as3(qa), as3(ka), as3(va))
            y1 = _mixer_b(as3(qb), as3(kb), as3(vb), _b_bias_table(rel_bias[jj]))
            w_out = even_w_out[jj]
        else:
            qc, kc, vc, fcum, qd, kd, vd = _odd_proj(
                h, g_mix, _odd_weights(odd_w_in[jj]), cos, sin,
                _row(forget_b[jj], LANES), seq)
            fq = as3(fcum)
            fk = jnp.swapaxes(fq[:, :, :N_HEADS], 1, 2)
            y0 = _mixer_c(as3(qc), as3(kc), as3(vc), fq, fk)
            y1 = _mixer_d(as3(qd), as3(kd), as3(vd), sinks[jj])
            w_out = odd_w_out[jj]
        w_out = w_out.astype(BF16)
        h = _out_proj(h, y0.reshape(m, HW), y1.reshape(m, HW), w_out[:HW], w_out[HW:], seq)
        h = _ffn(h, _row(norm_ffn_g[layer]), ffn_w_up[layer].astype(BF16),
                 ffn_conv_w[layer].astype(F32), _row(ffn_conv_b[layer]),
                 ffn_w_down[layer].astype(BF16), _row(norm_out_g), seq,
                 final_norm=(layer == depth - 1))
    return h.reshape(bsz, seq, D_MODEL)
```

```python
import functools

import numpy as np
import jax
import jax.numpy as jnp
from jax import lax
from jax.experimental import pallas as pl
from jax.experimental.pallas import tpu as pltpu

F32 = jnp.float32
BF16 = jnp.bfloat16

D_MODEL = 1024
HEAD_DIM = 64
CHUNK = 64
ROPE_THETA = 10000.0
EPS = 1e-6
NEG = -1e30
N_HEADS = 8
IDX_HEADS = 4
IDX_DIM = 64
IDX_TOPK_MAX = 256
B_PAST_CHUNKS = 8
B_REL_CLIP = 256
D_KV_HEADS = 2
D_WINDOW_CHUNKS = 2
D_FF = 2816
HW = N_HEADS * HEAD_DIM
SCALE = HEAD_DIM ** -0.5
LOG2E = 1.4426950408889634
QSCALE = SCALE * LOG2E

LANES = 128
VMEM_LIMIT = 56 * 1024 * 1024

TM_PROJ = 512
TM_FFN = 1024
TQ = 256
TQ_C = 256
SUB = 512
TKB_MAX = 2048
TF = 256
HALO = 16
LAZY_MARGIN = 20.0


def _cparams(n_axes):
    return pltpu.CompilerParams(
        dimension_semantics=("arbitrary",) * n_axes,
        vmem_limit_bytes=VMEM_LIMIT)


def _rms(x, g):
    ms = jnp.mean(x * x, axis=-1, keepdims=True)
    return x * lax.rsqrt(ms + EPS) * g


def _lane_iota(shape):
    return lax.broadcasted_iota(jnp.int32, shape, len(shape) - 1)


def _rope(t, cos, sin):
    w = t.shape[1]
    lane = _lane_iota(t.shape)
    fwd = pltpu.roll(t, 32, 1)
    bwd = pltpu.roll(t, w - 32, 1)
    rot = jnp.where((lane % HEAD_DIM) < (HEAD_DIM // 2), bwd, fwd)
    return t * cos + rot * sin


def _tile_lanes(t, reps):
    return t if reps == 1 else jnp.concatenate([t] * reps, axis=1)


def _head_pieces(p, sw=None):
    lane = _lane_iota(p.shape)
    lo = lane < HEAD_DIM
    q = p if sw is None else sw
    even = jnp.where(lane == HEAD_DIM, 1.0, jnp.where(lo, p, 0.0))
    odd = jnp.where(lane == 0, 1.0, jnp.where(lo, 0.0, q))
    return even, odd


def _expand_heads(t, ones_lane=False):
    pieces = []
    for j in range(t.shape[1] // LANES):
        p = t[:, j * LANES:(j + 1) * LANES]
        if ones_lane:
            pieces.extend(_head_pieces(p))
        else:
            lo = _lane_iota(p.shape) < HEAD_DIM
            pieces.append(jnp.where(lo, p, 0.0))
            pieces.append(jnp.where(lo, 0.0, p))
    return jnp.concatenate(pieces, axis=1).astype(BF16)


def _expand_kv_pair(t, ones_lane=False):
    lo = _lane_iota(t.shape) < HEAD_DIM
    sw = pltpu.roll(t, HEAD_DIM, 1)
    if ones_lane:
        e0, o0 = _head_pieces(t, sw)
        e1, o1 = _head_pieces(sw, t)
        pieces = [e0, o0, e1, o1]
    else:
        pieces = [jnp.where(lo, t, 0.0), jnp.where(lo, 0.0, sw),
                  jnp.where(lo, sw, 0.0), jnp.where(lo, 0.0, t)]
    return jnp.concatenate(pieces, axis=1).astype(BF16)


def _dot(a, b):
    return jnp.dot(a, b, preferred_element_type=F32)


def _dot_nt(a, b):
    return lax.dot_general(a, b, (((1,), (1,)), ((), ())), preferred_element_type=F32)


def _even_proj_kernel(x_ref, g_ref, w_ref, cos_ref, sin_ref, lng_ref, lnb_ref,
                      qa_ref, ka_ref, va_ref, qi_ref, ki_ref, wi_ref,
                      qb_ref, kb_ref, vb_ref):
    xn = _rms(x_ref[...], g_ref[...]).astype(BF16)
    cos = cos_ref[...]
    sin = sin_ref[...]
    cos4 = _tile_lanes(cos, 4)
    sin4 = _tile_lanes(sin, 4)

    def proj(a, b):
        return _dot(xn, w_ref[:, a:b])

    qa_ref[...] = (_rope(proj(0, 512), cos4, sin4) * QSCALE).astype(BF16)
    ka_ref[...] = _expand_heads(_rope(proj(512, 1024), cos4, sin4))
    va_ref[...] = _expand_heads(proj(1024, 1536), ones_lane=True)
    qi_ref[...] = _rope(proj(1536, 1792), _tile_lanes(cos, 2), _tile_lanes(sin, 2)).astype(BF16)
    small = proj(1792, 2048)
    kraw = small[:, :LANES]
    wi_ref[...] = small[:, LANES:]
    valid = _lane_iota(kraw.shape) < IDX_DIM
    mean = jnp.sum(kraw, axis=-1, keepdims=True) * (1.0 / IDX_DIM)
    xc = jnp.where(valid, kraw - mean, 0.0)
    var = jnp.sum(xc * xc, axis=-1, keepdims=True) * (1.0 / IDX_DIM)
    kn = xc * lax.rsqrt(var + EPS) * lng_ref[...] + lnb_ref[...]
    ki_ref[...] = _rope(kn, cos, sin).astype(BF16)
    qb_ref[...] = (proj(2048, 2560) * QSCALE).astype(BF16)
    kb_ref[...] = _expand_heads(proj(2560, 3072))
    vb_ref[...] = _expand_heads(proj(3072, 3584), ones_lane=True)


def _even_proj(x2, g, w, cos, sin, lng, lnb, seq):
    m = x2.shape[0]
    tm = min(TM_PROJ, seq)
    nseq = seq // tm
    row = lambda i: (i, 0)
    const = lambda i: (0, 0)
    tab = lambda i: (i % nseq, 0)
    widths = [(512, BF16), (1024, BF16), (1024, BF16), (256, BF16), (128, BF16),
              (128, F32), (512, BF16), (1024, BF16), (1024, BF16)]
    return pl.pallas_call(
        _even_proj_kernel,
        grid=(m // tm,),
        in_specs=[pl.BlockSpec((tm, D_MODEL), row),
                  pl.BlockSpec((1, D_MODEL), const),
                  pl.BlockSpec(w.shape, const),
                  pl.BlockSpec((tm, LANES), tab),
                  pl.BlockSpec((tm, LANES), tab),
                  pl.BlockSpec((1, LANES), const),
                  pl.BlockSpec((1, LANES), const)],
        out_specs=[pl.BlockSpec((tm, wd), row) for wd, _ in widths],
        out_shape=[jax.ShapeDtypeStruct((m, wd), dt) for wd, dt in widths],
        compiler_params=_cparams(1),
        name="even_proj",
    )(x2, g, w, cos, sin, lng, lnb)


def _odd_proj_kernel(x_ref, g_ref, w_ref, cos_ref, sin_ref, fb_ref, tril_ref,
                     qc_ref, kc_ref, vc_ref, f_ref, qd_ref, kd_ref, vd_ref,
                     carry_ref, *, nseq):
    i = pl.program_id(0)
    xn = _rms(x_ref[...], g_ref[...]).astype(BF16)
    cos = cos_ref[...]
    sin = sin_ref[...]

    def proj(a, b):
        return _dot(xn, w_ref[:, a:b])

    qc_ref[...] = (proj(0, 512) * QSCALE).astype(BF16)
    kc_ref[...] = _expand_heads(proj(512, 1024))
    vc_ref[...] = _expand_heads(proj(1024, 1536), ones_lane=True)

    z = proj(1536, 1664) + fb_ref[...]
    lf = jnp.minimum(z, 0.0) - jnp.log(1.0 + jnp.exp(-jnp.abs(z)))
    lf = jnp.where(_lane_iota(lf.shape) < N_HEADS, lf, 0.0)
    hi = lf.astype(BF16)
    r1 = lf - hi.astype(F32)
    mid = r1.astype(BF16)
    lo = (r1 - mid.astype(F32)).astype(BF16)
    tril = tril_ref[...]
    csum = _dot(tril, hi) + _dot(tril, mid) + _dot(tril, lo)

    @pl.when(i % nseq == 0)
    def _():
        carry_ref[...] = jnp.zeros_like(carry_ref)

    fcum = csum + carry_ref[0:1, :]
    f_ref[...] = fcum * LOG2E
    carry_ref[...] = jnp.broadcast_to(fcum[fcum.shape[0] - 1:, :], carry_ref.shape)

    qd_ref[...] = (_rope(proj(1664, 2176), _tile_lanes(cos, 4), _tile_lanes(sin, 4))
                   * QSCALE).astype(BF16)
    kv = proj(2176, 2432)
    kd_ref[...] = _expand_kv_pair(_rope(kv[:, :LANES], cos, sin))
    vd_ref[...] = _expand_kv_pair(kv[:, LANES:], ones_lane=True)


def _odd_proj(x2, g, w, cos, sin, fb, seq):
    m = x2.shape[0]
    tm = min(TM_PROJ, seq)
    nseq = seq // tm
    row = lambda i: (i, 0)
    const = lambda i: (0, 0)
    tab = lambda i: (i % nseq, 0)
    tril = jnp.tril(jnp.ones((tm, tm), F32)).astype(BF16)
    widths = [(512, BF16), (1024, BF16), (1024, BF16), (128, F32),
              (512, BF16), (512, BF16), (512, BF16)]
    return pl.pallas_call(
        functools.partial(_odd_proj_kernel, nseq=nseq),
        grid=(m // tm,),
        in_specs=[pl.BlockSpec((tm, D_MODEL), row),
                  pl.BlockSpec((1, D_MODEL), const),
                  pl.BlockSpec(w.shape, const),
                  pl.BlockSpec((tm, LANES), tab),
                  pl.BlockSpec((tm, LANES), tab),
                  pl.BlockSpec((1, LANES), const),
                  pl.BlockSpec((tm, tm), const)],
        out_specs=[pl.BlockSpec((tm, wd), row) for wd, _ in widths],
        out_shape=[jax.ShapeDtypeStruct((m, wd), dt) for wd, dt in widths],
        scratch_shapes=[pltpu.VMEM((8, LANES), F32)],
        compiler_params=_cparams(1),
        name="odd_proj",
    )(x2, g, w, cos, sin, fb, tril)


def _denominator_lane(h):
    return HEAD_DIM if h % 2 == 0 else 0


def _flash_heads(q_ref, k_ref, v_ref, rows, acc_ref, m_ref, logit_fn):
    nrep = SUB // LANES
    s_next = _head_scores(q_ref, k_ref, rows, 0)
    for h in range(N_HEADS):
        s_raw = s_next
        if h + 1 < N_HEADS:
            s_next = _head_scores(q_ref, k_ref, rows, h + 1)
        s, shift = logit_fn(h, s_raw)
        m_prev = m_ref[h]
        m_cur = jnp.max(s, axis=1, keepdims=True)
        if shift is not None:
            m_cur = m_cur + shift
        m_new = jnp.maximum(m_prev, m_cur)
        alpha = jnp.exp2(m_prev - m_new)
        off = m_new if shift is None else m_new - shift
        p = jnp.exp2(s - _tile_lanes(off, nrep))
        m_ref[h] = m_new
        cols = slice(h * LANES, (h + 1) * LANES)
        acc_ref[:, cols] = acc_ref[:, cols] * alpha + _dot(p.astype(BF16), v_ref[rows, cols])


def _head_scores(q_ref, k_ref, rows, h):
    jp = h // 2
    return _dot_nt(q_ref[:, jp * LANES:(jp + 1) * LANES], k_ref[rows, h * LANES:(h + 1) * LANES])


def _flash_heads_lazy(q_ref, k_ref, v_ref, rows, acc_ref, m_ref, tmp_ref, logit_fn):
    nrep = SUB // LANES
    worst = None
    pending = [_head_scores(q_ref, k_ref, rows, h) for h in range(2)]
    for h in range(N_HEADS):
        if h + 2 < N_HEADS:
            pending.append(_head_scores(q_ref, k_ref, rows, h + 2))
        s, shift = logit_fn(h, pending.pop(0))
        ref = m_ref[h] if shift is None else m_ref[h] - shift
        lane_max = functools.reduce(
            jnp.maximum, [s[:, t * LANES:(t + 1) * LANES] for t in range(nrep)])
        over = lane_max - ref
        worst = over if worst is None else jnp.maximum(worst, over)
        p = jnp.exp2(s - _tile_lanes(ref, nrep))
        cols = slice(h * LANES, (h + 1) * LANES)
        tmp_ref[:, cols] = _dot(p.astype(BF16), v_ref[rows, cols])
    exceeded = jnp.max(worst) > LAZY_MARGIN

    @pl.when(jnp.logical_not(exceeded))
    def _():
        acc_ref[...] += tmp_ref[...]

    @pl.when(exceeded)
    def _():
        _flash_heads(q_ref, k_ref, v_ref, rows, acc_ref, m_ref, logit_fn)


def _flash_init(acc_ref, m_ref):
    acc_ref[...] = jnp.zeros_like(acc_ref)
    m_ref[...] = jnp.full(m_ref.shape, NEG, F32)


def _pair_output(even, odd, l_even, l_odd):
    lo = _lane_iota(even.shape) < HEAD_DIM
    return jnp.where(lo, even / l_even, odd / l_odd)


def _flash_finish(o_ref, acc_ref):
    for jp in range(N_HEADS // 2):
        even = acc_ref[:, (2 * jp) * LANES:(2 * jp + 1) * LANES]
        odd = acc_ref[:, (2 * jp + 1) * LANES:(2 * jp + 2) * LANES]
        le = even[:, _denominator_lane(0):_denominator_lane(0) + 1]
        lo_ = odd[:, _denominator_lane(1):_denominator_lane(1) + 1]
        o_ref[:, jp * LANES:(jp + 1) * LANES] = _pair_output(even, odd, le, lo_).astype(o_ref.dtype)


def _kv_tiles(seq):
    tkb = min(TKB_MAX, seq)
    return tkb, seq // tkb, tkb // SUB


def _mixer_c_kernel(q_ref, k_ref, v_ref, fq_ref, fk_ref, o_ref, acc_ref, m_ref, fqr_ref,
                    *, tq, tkb, nsub):
    i = pl.program_id(1)
    j = pl.program_id(2)
    q_lo = i * tq
    last_sub = (q_lo + tq - 1) // SUB
    first_masked = q_lo // SUB
    jmax = last_sub // nsub

    @pl.when(j == 0)
    def _():
        _flash_init(acc_ref, m_ref)
        for h in range(N_HEADS):
            fqr_ref[h] = jnp.broadcast_to(fq_ref[:, h:h + 1], (tq, LANES))

    def step(c, masked):
        rows = pl.ds(pl.multiple_of(c * SUB, SUB), SUB)

        def logit_fn(h, s):
            s = s - fk_ref[h:h + 1, rows]
            if masked:
                kpos = j * tkb + c * SUB + lax.broadcasted_iota(jnp.int32, s.shape, 1)
                qpos = q_lo + lax.broadcasted_iota(jnp.int32, s.shape, 0)
                s = jnp.where(kpos <= qpos, s, NEG)
            return s, fqr_ref[h]

        _flash_heads(q_ref, k_ref, v_ref, rows, acc_ref, m_ref, logit_fn)

    @pl.when(j <= jmax)
    def _():
        lo = j * nsub
        n_full = jnp.clip(first_masked - lo, 0, nsub)
        n_all = jnp.clip(last_sub + 1 - lo, 0, nsub)

        def body(c, carry):
            step(c, False)
            return carry

        def masked_body(c, carry):
            step(c, True)
            return carry

        def group(g, carry):
            for n in range(PASS_UNROLL):
                step(g * PASS_UNROLL + n, False)
            return carry

        lax.fori_loop(0, n_full // PASS_UNROLL, group, 0)
        lax.fori_loop(PASS_UNROLL * (n_full // PASS_UNROLL), n_full, body, 0)
        lax.fori_loop(n_full, n_all, masked_body, 0)

        @pl.when(j == jmax)
        def _():
            _flash_finish(o_ref, acc_ref)


def _mixer_c(q, k, v, fq, fk):
    bsz, seq, _ = q.shape
    tq = min(TQ_C, seq)
    tkb, nkb, nsub = _kv_tiles(seq)
    kv_idx = lambda b, i, j: (b, jnp.minimum(j, (i * tq + tq - 1) // tkb), 0)
    return pl.pallas_call(
        functools.partial(_mixer_c_kernel, tq=tq, tkb=tkb, nsub=nsub),
        grid=(bsz, seq // tq, nkb),
        in_specs=[pl.BlockSpec((None, tq, HW), lambda b, i, j: (b, i, 0)),
                  pl.BlockSpec((None, tkb, 2 * HW), kv_idx),
                  pl.BlockSpec((None, tkb, 2 * HW), kv_idx),
                  pl.BlockSpec((None, tq, LANES), lambda b, i, j: (b, i, 0)),
                  pl.BlockSpec((None, N_HEADS, tkb),
                               lambda b, i, j: (b, 0, jnp.minimum(j, (i * tq + tq - 1) // tkb)))],
        out_specs=pl.BlockSpec((None, tq, HW), lambda b, i, j: (b, i, 0)),
        out_shape=jax.ShapeDtypeStruct((bsz, seq, HW), BF16),
        scratch_shapes=[pltpu.VMEM((tq, 2 * HW), F32),
                        pltpu.VMEM((N_HEADS, tq, LANES), F32),
                        pltpu.VMEM((N_HEADS, tq, LANES), F32)],
        compiler_params=_cparams(3),
        name="mixer_c",
    )(q, k, v, fq, fk)


MIN_NORMAL_KEY = 0x00800000
BLIND_PASSES = 10
COUNT_UNROLL = 4
PASS_UNROLL = 4


def _key_flip(bits):
    return jnp.where(bits >= 0, bits, bits ^ jnp.int32(0x7FFFFFFF))


def _float_to_key(f):
    return _key_flip(pltpu.bitcast(f, jnp.int32))


def _key_to_float(k):
    return pltpu.bitcast(_key_flip(k), F32)


def _mixer_a_kernel(qi_ref, wi_ref, kie_ref, kio_ref, q_ref, k_ref, v_ref, tri_ref,
                    o_ref, s_ref, gmax_ref, cnt_ref, acc_ref, m_ref, tmp_ref,
                    *, tkb, nsub, topk):
    i = pl.program_id(1)
    j = pl.program_id(2)
    q_lo = i * TQ
    last_sub = q_lo // SUB
    jmax = last_sub // nsub
    n_sub = last_sub + 1
    nrep = SUB // LANES
    half = TQ // 2

    def admissible(c_glob, shape):
        kpos = c_glob * SUB + lax.broadcasted_iota(jnp.int32, shape, 1)
        qpos = q_lo + lax.broadcasted_iota(jnp.int32, shape, 0)
        return (kpos // CHUNK) <= (qpos // CHUNK)

    def s_load(c):
        return jnp.concatenate([s_ref[c * nrep + t] for t in range(nrep)], axis=1)

    def s_store(c, val):
        for t in range(nrep):
            s_ref[c * nrep + t] = val[:, t * LANES:(t + 1) * LANES]

    def masked_logits(c_glob):
        return lambda h, s: (s + s_load(c_glob), None)

    @pl.when(j == 0)
    def _():
        _flash_init(acc_ref, m_ref)
        cnt_ref[...] = jnp.zeros_like(cnt_ref)
        gmax_ref[...] = jnp.full(gmax_ref.shape, -jnp.inf, F32)

        def score(c, masked):
            cols = pl.ds(pl.multiple_of(c * SUB, SUB), SUB)
            tot = None
            for pair in range(IDX_HEADS // 2):
                qp = qi_ref[:, pair * LANES:(pair + 1) * LANES]
                for e in range(2):
                    h = 2 * pair + e
                    kt = (kie_ref if e == 0 else kio_ref)[:, cols]
                    term = jnp.maximum(_dot(qp, kt), 0.0) * wi_ref[:, h:h + 1]
                    tot = term if tot is None else tot + term
            if masked:
                tot = jnp.where(admissible(c, tot.shape), tot, NEG)
            s_store(c, tot)
            for par in range(2):
                g = gmax_ref[par]
                for t in range(par, nrep, 2):
                    g = jnp.maximum(g, tot[:, t * LANES:(t + 1) * LANES])
                gmax_ref[par] = g

        def for_each_full_sub_tile(fn):
            u = PASS_UNROLL

            def group(g, carry):
                for n in range(u):
                    fn(g * u + n)
                return carry

            def single(c, carry):
                fn(c)
                return carry

            lax.fori_loop(0, last_sub // u, group, 0)
            lax.fori_loop(u * (last_sub // u), last_sub, single, 0)

        for_each_full_sub_tile(lambda c: score(c, False))
        score(last_sub, True)

        def count(cand):
            accs = []
            for hh in range(2):
                r0 = hh * half
                ch = cand[r0:r0 + half]

                def scan(t0, width, acc):
                    for t in range(width):
                        blk = s_ref[t0 + t, r0:r0 + half, :]
                        acc = acc + jnp.where(blk >= ch, 1.0, 0.0)
                    return acc

                u = COUNT_UNROLL
                acc = lax.fori_loop(0, n_sub // u, lambda c, a: scan(c * u * nrep, u * nrep, a),
                                    jnp.zeros((half, LANES), F32))
                acc = lax.fori_loop(u * (n_sub // u), n_sub, lambda c, a: scan(c * nrep, nrep, a), acc)
                accs.append(acc)
            total = jnp.sum(jnp.concatenate(accs, axis=0), axis=1, keepdims=True)
            return jnp.broadcast_to(total, (TQ, LANES))

        ga = gmax_ref[0]
        gb = gmax_ref[1]
        lo_f = jnp.min(jnp.minimum(ga, gb), axis=1, keepdims=True)
        hi_f = jnp.max(jnp.maximum(ga, gb), axis=1, keepdims=True)
        lo0 = jnp.broadcast_to(_float_to_key(lo_f), (TQ, LANES))
        hi0 = jnp.broadcast_to(_float_to_key(hi_f), (TQ, LANES))

        def unresolved(st):
            lo, hi, _ = st
            return jnp.max(jnp.where(lo < hi, 1.0, 0.0)) > 0.5

        def narrow(st, mid):
            lo, hi, n_above = st
            n_ge = count(_key_to_float(mid))
            ge = n_ge >= topk
            exact = n_ge == topk
            below = jnp.where(mid == MIN_NORMAL_KEY, 0, mid - 1)
            return (jnp.where(ge, mid, lo),
                    jnp.where(exact, mid, jnp.where(ge, hi, below)),
                    jnp.where(exact, 0.0, jnp.where(ge, n_above, n_ge)))

        def bisect(st):
            lo, hi, _ = st
            by_key = (lo >> 1) + (hi >> 1) + ((lo | hi) & 1)
            lo_f = _key_to_float(lo)
            hi_f = _key_to_float(hi)
            by_val = _float_to_key(lo_f + (hi_f - lo_f) * 0.5)
            by_val = jnp.minimum(jnp.maximum(by_val, lo + 1), hi)
            same_binade = (pltpu.bitcast(lo_f, jnp.int32) >> 23) == (pltpu.bitcast(hi_f, jnp.int32) >> 23)
            mid = jnp.where(same_binade, by_key, by_val)
            mid = jnp.where(lo < 0, jnp.where(hi >= 0, 0, mid), mid)
            mid = jnp.where(lo == 0, jnp.where(hi >= MIN_NORMAL_KEY, MIN_NORMAL_KEY, mid), mid)
            mid = jnp.where(lo < hi, mid, lo)
            return narrow(st, mid)

        state = narrow((lo0, hi0, jnp.zeros((TQ, LANES), F32)), hi0)
        state = lax.fori_loop(0, BLIND_PASSES, lambda _, st: bisect(st), state)
        lo, _, n_above = lax.while_loop(unresolved, bisect, state)
        tau = _tile_lanes(_key_to_float(lo), nrep)
        keep = _tile_lanes(topk - n_above, nrep)

        def to_mask(c, masked):
            blk = s_load(c)
            eq = blk == tau
            rank = (_tile_lanes(cnt_ref[...], nrep)
                    + _dot(jnp.where(eq, 1.0, 0.0).astype(BF16), tri_ref[...]))
            keep_tie = jnp.where(rank <= keep, 0.0, NEG)
            bias = jnp.where(blk > tau, 0.0, jnp.where(eq, keep_tie, NEG))
            if masked:
                bias = jnp.where(admissible(c, bias.shape), bias, NEG)
            cnt_ref[...] = jnp.broadcast_to(rank[:, SUB - 1:SUB], (TQ, LANES))
            s_store(c, bias)

        for_each_full_sub_tile(lambda c: to_mask(c, False))
        to_mask(last_sub, True)

        _flash_heads(q_ref, k_ref, v_ref, pl.ds(0, SUB), acc_ref, m_ref, masked_logits(0))

    @pl.when(j <= jmax)
    def _():
        def body(c, carry):
            rows = pl.ds(pl.multiple_of(c * SUB, SUB), SUB)
            _flash_heads_lazy(q_ref, k_ref, v_ref, rows, acc_ref, m_ref, tmp_ref,
                              masked_logits(j * nsub + c))
            return carry

        lax.fori_loop(jnp.where(j == 0, 1, 0), jnp.minimum(n_sub - j * nsub, nsub), body, 0)

        @pl.when(j == jmax)
        def _():
            _flash_finish(o_ref, acc_ref)


def _mixer_a(qi, wi, kie, kio, q, k, v):
    bsz, seq, _ = q.shape
    tkb, nkb, nsub = _kv_tiles(seq)
    topk = min(IDX_TOPK_MAX, seq // 4)
    tri = jnp.triu(jnp.ones((SUB, SUB), F32)).astype(BF16)
    qmap = lambda b, i, j: (b, i, 0)
    kv_idx = lambda b, i, j: (b, jnp.minimum(j, (i * TQ) // tkb), 0)
    res = lambda b, i, j: (b, 0, 0)
    return pl.pallas_call(
        functools.partial(_mixer_a_kernel, tkb=tkb, nsub=nsub, topk=topk),
        grid=(bsz, seq // TQ, nkb),
        in_specs=[pl.BlockSpec((None, TQ, IDX_HEADS * IDX_DIM), qmap),
                  pl.BlockSpec((None, TQ, LANES), qmap),
                  pl.BlockSpec((None, LANES, seq), res),
                  pl.BlockSpec((None, LANES, seq), res),
                  pl.BlockSpec((None, TQ, HW), qmap),
                  pl.BlockSpec((None, tkb, 2 * HW), kv_idx),
                  pl.BlockSpec((None, tkb, 2 * HW), kv_idx),
                  pl.BlockSpec((SUB, SUB), lambda b, i, j: (0, 0))],
        out_specs=pl.BlockSpec((None, TQ, HW), qmap),
        out_shape=jax.ShapeDtypeStruct((bsz, seq, HW), BF16),
        scratch_shapes=[pltpu.VMEM((seq // LANES, TQ, LANES), F32),
                        pltpu.VMEM((2, TQ, LANES), F32),
                        pltpu.VMEM((TQ, LANES), F32),
                        pltpu.VMEM((TQ, 2 * HW), F32),
                        pltpu.VMEM((N_HEADS, TQ, LANES), F32),
                        pltpu.VMEM((TQ, 2 * HW), F32)],
        compiler_params=_cparams(3),
        name="mixer_a",
    )(qi, wi, kie, kio, q, k, v, tri)


B_WIN = 3


def _window_heads(q_ref, o_ref, n_win, k_of, v_of, bias_of, sink_of=None):
    def scores(h):
        jp = h // 2
        qp = q_ref[:, jp * LANES:(jp + 1) * LANES]
        return [_dot_nt(qp, k_of(h, a)) + bias_of(h, a) for a in range(n_win)]

    s_next = scores(0)
    done = []
    for h in range(N_HEADS):
        s = s_next
        if h + 1 < N_HEADS:
            s_next = scores(h + 1)
        lane_max = functools.reduce(
            jnp.maximum, [t[:, n * LANES:(n + 1) * LANES] for t in s for n in range(t.shape[1] // LANES)])
        m = jnp.max(lane_max, axis=1, keepdims=True)
        if sink_of is not None:
            m = jnp.maximum(m, sink_of(h))
        m_rep = jnp.broadcast_to(m, lane_max.shape)
        pv = functools.reduce(
            jnp.add, [_dot(jnp.exp2(s[a] - _tile_lanes(m_rep, s[a].shape[1] // LANES)).astype(BF16),
                           v_of(h, a)) for a in range(n_win)])
        lane = _denominator_lane(h)
        l = pv[:, lane:lane + 1]
        if sink_of is not None:
            l = l + jnp.exp2(sink_of(h) - m)
        done.append((pv, l))
        if h % 2 == 1:
            (pe, le), (po, lo_) = done[-2], done[-1]
            jp = h // 2
            o_ref[:, jp * LANES:(jp + 1) * LANES] = _pair_output(pe, po, le, lo_).astype(o_ref.dtype)


def _mixer_b_kernel(q_ref, k0_ref, k1_ref, k2_ref, v0_ref, v1_ref, v2_ref, bias_ref, o_ref):
    i = pl.program_id(1)
    k_refs = (k0_ref, k1_ref, k2_ref)
    v_refs = (v0_ref, v1_ref, v2_ref)
    pad = [jnp.where(i - (B_WIN - 1 - a) >= 0, 0.0, NEG) for a in range(B_WIN)]
    hc = lambda h: slice(h * LANES, (h + 1) * LANES)
    _window_heads(q_ref, o_ref, B_WIN,
                  lambda h, a: k_refs[a][:, hc(h)],
                  lambda h, a: v_refs[a][:, hc(h)],
                  lambda h, a: bias_ref[h, :, a * TQ:(a + 1) * TQ] + pad[a])


def _mixer_b(q, k, v, bias_tab):
    bsz, seq, _ = q.shape
    qmap = lambda b, i: (b, i, 0)
    kmaps = [functools.partial(lambda b, i, d: (b, jnp.maximum(i - d, 0), 0), d=B_WIN - 1 - a)
             for a in range(B_WIN)]
    kspec = [pl.BlockSpec((None, TQ, 2 * HW), km) for km in kmaps]
    return pl.pallas_call(
        _mixer_b_kernel,
        grid=(bsz, seq // TQ),
        in_specs=[pl.BlockSpec((None, TQ, HW), qmap)] + kspec + kspec
                 + [pl.BlockSpec(bias_tab.shape, lambda b, i: (0, 0, 0))],
        out_specs=pl.BlockSpec((None, TQ, HW), qmap),
        out_shape=jax.ShapeDtypeStruct((bsz, seq, HW), BF16),
        compiler_params=_cparams(2),
        name="mixer_b",
    )(q, k, k, k, v, v, v, bias_tab)


def _b_bias_table(rel_bias):
    win = B_WIN * TQ
    span = TQ + win
    u = np.arange(span)
    u = np.where(u >= win, u - span, u)
    idx = np.clip((win - TQ) - u, -B_REL_CLIP, B_REL_CLIP) + B_REL_CLIP
    v = rel_bias.astype(F32)[:, idx] * LOG2E
    tab = jnp.tile(v, (1, TQ))[:, :TQ * (span - 1)].reshape(N_HEADS, TQ, span - 1)[:, :, :win]
    t = np.arange(TQ)[:, None]
    jk = np.arange(win)[None, :]
    cdiff = (t // CHUNK + (B_WIN - 1) * TQ // CHUNK) - jk // CHUNK
    ok = (cdiff >= 0) & (cdiff <= B_PAST_CHUNKS)
    return jnp.where(jnp.asarray(ok)[None], tab, NEG)


def _mixer_d_kernel(q_ref, k0_ref, k1_ref, v0_ref, v1_ref, sink_ref, o_ref):
    i = pl.program_id(1)
    k_refs = (k0_ref, k1_ref)
    v_refs = (v0_ref, v1_ref)
    qpos = TQ + lax.broadcasted_iota(jnp.int32, (TQ, TQ), 0)
    bias = []
    for a in range(2):
        kpos = a * TQ + lax.broadcasted_iota(jnp.int32, (TQ, TQ), 1)
        cdiff = qpos // CHUNK - kpos // CHUNK
        ok = (cdiff >= 0) & (cdiff <= D_WINDOW_CHUNKS)
        if a == 0:
            ok = ok & (i > 0)
        bias.append(jnp.where(ok, 0.0, NEG))
    group = N_HEADS // D_KV_HEADS

    def piece(h):
        n = (h // group) * 2 + h % 2
        return slice(n * LANES, (n + 1) * LANES)

    _window_heads(q_ref, o_ref, 2,
                  lambda h, a: k_refs[a][:, piece(h)],
                  lambda h, a: v_refs[a][:, piece(h)],
                  lambda h, a: bias[a],
                  sink_of=lambda h: sink_ref[h:h + 1, 0:1])


def _mixer_d(q, k, v, sinks):
    bsz, seq, _ = q.shape
    qmap = lambda b, i: (b, i, 0)
    prev = lambda b, i: (b, jnp.maximum(i - 1, 0), 0)
    sink_tab = jnp.broadcast_to((sinks.astype(F32) * LOG2E)[:, None], (N_HEADS, LANES))
    return pl.pallas_call(
        _mixer_d_kernel,
        grid=(bsz, seq // TQ),
        in_specs=[pl.BlockSpec((None, TQ, HW), qmap),
                  pl.BlockSpec((None, TQ, HW), prev), pl.BlockSpec((None, TQ, HW), qmap),
                  pl.BlockSpec((None, TQ, HW), prev), pl.BlockSpec((None, TQ, HW), qmap),
                  pl.BlockSpec((N_HEADS, LANES), lambda b, i: (0, 0))],
        out_specs=pl.BlockSpec((None, TQ, HW), qmap),
        out_shape=jax.ShapeDtypeStruct((bsz, seq, HW), BF16),
        compiler_params=_cparams(2),
        name="mixer_d",
    )(q, k, k, v, v, sink_tab)


def _out_proj_kernel(h_ref, y0_ref, y1_ref, w0_ref, w1_ref, o_ref):
    o_ref[...] = h_ref[...] + _dot(y0_ref[...], w0_ref[...]) + _dot(y1_ref[...], w1_ref[...])


def _out_proj(h2, y0, y1, w0, w1, seq):
    m = h2.shape[0]
    tm = min(TM_PROJ, seq)
    row = lambda i: (i, 0)
    const = lambda i: (0, 0)
    return pl.pallas_call(
        _out_proj_kernel,
        grid=(m // tm,),
        in_specs=[pl.BlockSpec((tm, D_MODEL), row),
                  pl.BlockSpec((tm, HW), row), pl.BlockSpec((tm, HW), row),
                  pl.BlockSpec((HW, D_MODEL), const), pl.BlockSpec((HW, D_MODEL), const)],
        out_specs=pl.BlockSpec((tm, D_MODEL), row),
        out_shape=jax.ShapeDtypeStruct((m, D_MODEL), F32),
        compiler_params=_cparams(1),
        name="out_proj",
    )(h2, y0, y1, w0, w1)


def _ffn_kernel(h_ref, halo_ref, g_ref, wg_ref, wu_ref, cwg_ref, cwu_ref, cbg_ref, cbu_ref,
                wd_ref, go_ref, o_ref, xn_ref, xh_ref, acc_ref, hb_ref, *, nseq, final_norm):
    i = pl.program_id(0)
    f = pl.program_id(1)
    tm = h_ref.shape[0]

    @pl.when(f == 0)
    def _():
        xn_ref[...] = _rms(h_ref[...], g_ref[...]).astype(BF16)
        xh_ref[...] = _rms(halo_ref[...], g_ref[...]).astype(BF16)
        acc_ref[...] = jnp.zeros_like(acc_ref)

    halo_on = jnp.where(i % nseq == 0, 0.0, 1.0)

    def branch(w_ref, cw_ref, cb_ref):
        hb_ref[HALO:, :] = _dot(xn_ref[...], w_ref[...])
        hb_ref[:HALO, :] = _dot(xh_ref[...], w_ref[...]) * halo_on
        cw = cw_ref[...]
        return (hb_ref[pl.ds(HALO, tm), :] * cw[2:3, :]
                + hb_ref[pl.ds(HALO - 1, tm), :] * cw[1:2, :]
                + hb_ref[pl.ds(HALO - 2, tm), :] * cw[0:1, :]
                + cb_ref[...])

    yg = branch(wg_ref, cwg_ref, cbg_ref)
    yu = branch(wu_ref, cwu_ref, cbu_ref)
    act = (yg / (1.0 + jnp.exp(-yg))) * yu
    acc_ref[...] += _dot(act.astype(BF16), wd_ref[...])

    @pl.when(f == pl.num_programs(1) - 1)
    def _():
        out = h_ref[...] + acc_ref[...]
        if final_norm:
            out = _rms(out, go_ref[...])
        o_ref[...] = out


def _ffn(h2, g, w_up, conv_w, conv_b, w_down, g_out, seq, final_norm):
    m = h2.shape[0]
    tm = min(TM_FFN, seq)
    nseq = seq // tm
    nf = D_FF // TF
    row = lambda i, f: (i, 0)
    const = lambda i, f: (0, 0)
    halo = lambda i, f: (jnp.maximum(i * (tm // HALO) - 1, 0), 0)
    gcol = lambda i, f: (0, f)
    ucol = lambda i, f: (0, nf + f)
    return pl.pallas_call(
        functools.partial(_ffn_kernel, nseq=nseq, final_norm=final_norm),
        grid=(m // tm, nf),
        in_specs=[pl.BlockSpec((tm, D_MODEL), row),
                  pl.BlockSpec((HALO, D_MODEL), halo),
                  pl.BlockSpec((1, D_MODEL), const),
                  pl.BlockSpec((D_MODEL, TF), gcol), pl.BlockSpec((D_MODEL, TF), ucol),
                  pl.BlockSpec((3, TF), gcol), pl.BlockSpec((3, TF), ucol),
                  pl.BlockSpec((1, TF), gcol), pl.BlockSpec((1, TF), ucol),
                  pl.BlockSpec((TF, D_MODEL), lambda i, f: (f, 0)),
                  pl.BlockSpec((1, D_MODEL), const)],
        out_specs=pl.BlockSpec((tm, D_MODEL), row),
        out_shape=jax.ShapeDtypeStruct((m, D_MODEL), F32),
        scratch_shapes=[pltpu.VMEM((tm, D_MODEL), BF16),
                        pltpu.VMEM((HALO, D_MODEL), BF16),
                        pltpu.VMEM((tm, D_MODEL), F32),
                        pltpu.VMEM((tm + HALO, TF), F32)],
        compiler_params=_cparams(2),
        name="ffn",
    )(h2, h2, g, w_up, w_up, conv_w, conv_w, conv_b, conv_b, w_down, g_out)


def _rope_tables(seq):
    inv = 1.0 / (ROPE_THETA ** (jnp.arange(0, HEAD_DIM, 2, dtype=F32) / HEAD_DIM))
    ang = jnp.arange(seq, dtype=F32)[:, None] * inv[None, :]
    c, s = jnp.cos(ang), jnp.sin(ang)
    cos = jnp.concatenate([c, c, c, c], axis=1)
    sin = jnp.concatenate([-s, s, -s, s], axis=1)
    return cos, sin


def _pad_cols(w, width):
    return jnp.pad(w, ((0, 0), (0, width - w.shape[1])))


def _even_weights(w_in):
    o = np.cumsum([0, 512, 512, 512, IDX_HEADS * IDX_DIM, IDX_DIM, IDX_HEADS, 512, 512, 512])
    sl = [w_in[:, o[n]:o[n + 1]] for n in range(9)]
    small = jnp.concatenate([_pad_cols(sl[4], LANES), _pad_cols(sl[5], LANES)], axis=1)
    return jnp.concatenate([sl[0], sl[1], sl[2], sl[3], small, sl[6], sl[7], sl[8]],
                           axis=1).astype(BF16)


def _odd_weights(w_in):
    o = np.cumsum([0, 512, 512, 512, N_HEADS, 512, D_KV_HEADS * HEAD_DIM, D_KV_HEADS * HEAD_DIM])
    sl = [w_in[:, o[n]:o[n + 1]] for n in range(7)]
    return jnp.concatenate([sl[0], sl[1], sl[2], _pad_cols(sl[3], LANES), sl[4], sl[5], sl[6]],
                           axis=1).astype(BF16)


def _row(v, width=None):
    v = v.astype(F32)[None, :]
    return v if width is None else _pad_cols(v, width)


def kernel(x, norm_mix_g, norm_ffn_g, norm_out_g, even_w_in, even_w_out, idx_k_ln_g, idx_k_ln_b,
           rel_bias, odd_w_in, odd_w_out, forget_b, sinks, ffn_w_up, ffn_conv_w, ffn_conv_b,
           ffn_w_down):
    bsz, seq, _ = x.shape
    m = bsz * seq
    depth = norm_mix_g.shape[0]
    assert seq % TQ == 0 and seq % min(TM_PROJ, seq) == 0 and seq % min(TKB_MAX, seq) == 0
    cos, sin = _rope_tables(seq)
    h = x.reshape(m, D_MODEL)
    as3 = lambda t: t.reshape(bsz, seq, t.shape[-1])

    for layer in range(depth):
        jj = layer // 2
        g_mix = _row(norm_mix_g[layer])
        if layer % 2 == 0:
            qa, ka, va, qi, ki, wi, qb, kb, vb = _even_proj(
                h, g_mix, _even_weights(even_w_in[jj]), cos, sin,
                _row(idx_k_ln_g[jj], LANES), _row(idx_k_ln_b[jj], LANES), seq)
            kit = jnp.swapaxes(as3(ki), 1, 2)
            kie, kio = kit, jnp.roll(kit, IDX_DIM, axis=1)
            y0 = _mixer_a(as3(qi), as3(wi), kie, kio, as3(qa), as3(ka), as3(va))
            y1 = _mixer_b(as3(qb), as3(kb), as3(vb), _b_bias_table(rel_bias[jj]))
            w_out = even_w_out[jj]
        else:
            qc, kc, vc, fcum, qd, kd, vd = _odd_proj(
                h, g_mix, _odd_weights(odd_w_in[jj]), cos, sin,
                _row(forget_b[jj], LANES), seq)
            fq = as3(fcum)
            fk = jnp.swapaxes(fq[:, :, :N_HEADS], 1, 2)
            y0 = _mixer_c(as3(qc), as3(kc), as3(vc), fq, fk)
            y1 = _mixer_d(as3(qd), as3(kd), as3(vd), sinks[jj])
            w_out = odd_w_out[jj]
        w_out = w_out.astype(BF16)
        h = _out_proj(h, y0.reshape(m, HW), y1.reshape(m, HW), w_out[:HW], w_out[HW:], seq)
        h = _ffn(h, _row(norm_ffn_g[layer]), ffn_w_up[layer].astype(BF16),
                 ffn_conv_w[layer].astype(F32), _row(ffn_conv_b[layer]),
                 ffn_w_down[layer].astype(BF16), _row(norm_out_g), seq,
                 final_norm=(layer == depth - 1))
    return h.reshape(bsz, seq, D_MODEL)
```
